```python
import jax
import jax.numpy as jnp
from jax import lax
import numpy as np

D_MODEL = 1024
BATCH = 1
SEQ = 16384
DEPTH = 1
DEC_BATCH = 128
DEC_SEQ = 1
PAST_LEN = 8192
PAGE_SIZE = 128

HEAD_DIM = 64
H_A = D_MODEL // (2 * HEAD_DIM)
H_B = D_MODEL // (2 * HEAD_DIM)
W_A = H_A * HEAD_DIM
W_B = H_B * HEAD_DIM
MOBA_BLOCK = 256
MOBA_TOPK = 3
ROPE_THETA = 500000.0
ROT_DIM = HEAD_DIM // 4
Q_BLOCK = 128
D_FF = -(-8 * D_MODEL // (3 * 256)) * 256
RMS_EPS = 1e-6
NEG_INF = -1e30
IN_SPLITS = (W_A, W_A, W_A, W_B, W_B, W_B, H_B, D_MODEL, D_MODEL)
N_IN = 3 * W_A + 3 * W_B + H_B + 2 * D_MODEL

kernel_name = 'moba_fox_gated_hybrid_step'


def rmsnorm(x, g):
    xf = x.astype(jnp.float32)
    y = xf * lax.rsqrt(jnp.mean(xf * xf, axis=-1, keepdims=True) + RMS_EPS)
    return (y * g.astype(jnp.float32)).astype(x.dtype)


def adaln(c, w_ada, b_ada):
    mod = jax.nn.silu(c) @ w_ada + b_ada
    return jnp.split(mod[:, None, :], 6, axis=-1)


def modulate(x, g, shift, scale):
    return rmsnorm(x, g) * (1 + scale) + shift


def partial_rope(x, pos):
    half = ROT_DIM // 2
    inv = ROPE_THETA ** (-jnp.arange(half, dtype=jnp.float32) / half)
    ang = pos.astype(jnp.float32)[:, None] * inv[None, :]
    cos = jnp.cos(ang)[None, :, None, :]
    sin = jnp.sin(ang)[None, :, None, :]
    xf = x.astype(jnp.float32)
    x1, x2, rest = xf[..., :half], xf[..., half:ROT_DIM], xf[..., ROT_DIM:]
    out = jnp.concatenate([x1 * cos - x2 * sin, x2 * cos + x1 * sin, rest], axis=-1)
    return out.astype(x.dtype)


def project_in(u, pos, w_in, b_f):
    B, T, _ = u.shape
    cuts = [int(v) for v in np.cumsum(IN_SPLITS)[:-1]]
    qa, ka, va, qb, kb, vb, fl, ga, gb = jnp.split(u @ w_in, cuts, axis=-1)
    qa = partial_rope(qa.reshape(B, T, H_A, HEAD_DIM), pos)
    ka = partial_rope(ka.reshape(B, T, H_A, HEAD_DIM), pos)
    va = va.reshape(B, T, H_A, HEAD_DIM)
    qb = qb.reshape(B, T, H_B, HEAD_DIM)
    kb = kb.reshape(B, T, H_B, HEAD_DIM)
    vb = vb.reshape(B, T, H_B, HEAD_DIM)
    logf = jax.nn.log_sigmoid((fl + b_f).astype(jnp.float32))
    return qa, ka, va, qb, kb, vb, logf, ga, gb


def attend_segments(q, segs):
    scale = HEAD_DIM ** -0.5
    logits = [jnp.einsum('bqhd,blhd->bhql', q, k, preferred_element_type=jnp.float32) * scale + bias
              for k, _, bias in segs]
    p = jax.nn.softmax(jnp.concatenate(logits, axis=-1), axis=-1)
    out, start = None, 0
    for k, v, _ in segs:
        L = k.shape[1]
        term = jnp.einsum('bhql,blhd->bqhd', p[..., start:start + L].astype(v.dtype), v)
        out = term if out is None else out + term
        start += L
    return out.astype(q.dtype)


def fox_prompt(q, k, v, logf):
    B, T, H, Dh = q.shape
    F = jnp.cumsum(logf, axis=1).transpose(0, 2, 1)
    kpos = jnp.arange(T)

    def one_block(i):
        start = i * Q_BLOCK
        qb = lax.dynamic_slice_in_dim(q, start, Q_BLOCK, axis=1)
        Fq = lax.dynamic_slice_in_dim(F, start, Q_BLOCK, axis=2)
        qpos = start + jnp.arange(Q_BLOCK)
        bias = jnp.where(kpos[None, None, None, :] <= qpos[None, None, :, None],
                         Fq[..., None] - F[:, :, None, :], NEG_INF)
        return attend_segments(qb, [(k, v, bias)])

    out = lax.map(one_block, jnp.arange(T // Q_BLOCK))
    return out.transpose(1, 0, 2, 3, 4).reshape(B, T, H, Dh)


def fox_sample(q, k_new, v_new, logf_new, pool_k, pool_v, pool_logf, page_table, l):
    DB, DS, H, Dh = q.shape
    P = page_table.shape[1] * PAGE_SIZE
    k_past = pool_k[l, page_table].reshape(DB, P, H, Dh)
    v_past = pool_v[l, page_table].reshape(DB, P, H, Dh)
    lf_past = pool_logf[l, page_table].reshape(DB, P, H).astype(jnp.float32)
    Fp = jnp.cumsum(lf_past, axis=1)
    Fn = Fp[:, -1:, :] + jnp.cumsum(logf_new, axis=1)
    FpT = Fp.transpose(0, 2, 1)
    FnT = Fn.transpose(0, 2, 1)
    bias_past = FnT[..., None] - FpT[:, :, None, :]
    npos = jnp.arange(DS)
    bias_new = jnp.where(npos[None, None, None, :] <= npos[None, None, :, None],
                         FnT[..., None] - FnT[:, :, None, :], NEG_INF)
    return attend_segments(q, [(k_past, v_past, bias_past), (k_new, v_new, bias_new)])


def moba_select(q, qpos, kmean):
    nbf = kmean.shape[1]
    ks = min(MOBA_TOPK, nbf)
    s = jnp.einsum('bqhd,bnhd->bqhn', q, kmean, preferred_element_type=jnp.float32)
    npast = qpos // MOBA_BLOCK
    s = jnp.where(jnp.arange(nbf)[None, None, None, :] < npast[None, :, None, None], s, NEG_INF)
    _, idx = lax.top_k(s, ks)
    valid = jnp.arange(ks)[None, None, None, :] < jnp.minimum(npast, ks)[None, :, None, None]
    return idx, valid


def moba_combine(q, sel, own_k, own_v, own_mask):
    scale = HEAD_DIM ** -0.5
    s_own = jnp.einsum('bqhd,blhd->bqhl', q, own_k, preferred_element_type=jnp.float32) * scale
    s_own = jnp.where(own_mask[None, :, None, :], s_own, NEG_INF)
    if sel is None:
        p = jax.nn.softmax(s_own, axis=-1)
        return jnp.einsum('bqhl,blhd->bqhd', p.astype(own_v.dtype), own_v).astype(q.dtype)
    sel_k, sel_v, valid = sel
    B, Q, H, KS, BL, _ = sel_k.shape
    s_sel = jnp.einsum('bqhd,bqhkld->bqhkl', q, sel_k, preferred_element_type=jnp.float32) * scale
    s_sel = jnp.where(valid[..., None], s_sel, NEG_INF).reshape(B, Q, H, KS * BL)
    p = jax.nn.softmax(jnp.concatenate([s_sel, s_own], axis=-1), axis=-1)
    p_sel = p[..., :KS * BL].reshape(B, Q, H, KS, BL).astype(sel_v.dtype)
    p_own = p[..., KS * BL:].astype(own_v.dtype)
    out = (jnp.einsum('bqhkl,bqhkld->bqhd', p_sel, sel_v)
           + jnp.einsum('bqhl,blhd->bqhd', p_own, own_v))
    return out.astype(q.dtype)


def moba_prompt(q, k, v):
    B, T, H, Dh = q.shape
    nbf = T // MOBA_BLOCK
    nbc = -(-T // MOBA_BLOCK)
    pad = ((0, 0), (0, nbc * MOBA_BLOCK - T), (0, 0), (0, 0))
    kblk = jnp.pad(k, pad).reshape(B, nbc, MOBA_BLOCK, H, Dh)
    vblk = jnp.pad(v, pad).reshape(B, nbc, MOBA_BLOCK, H, Dh)
    kmean = jnp.mean(kblk[:, :nbf].astype(jnp.float32), axis=2).astype(k.dtype)
    bi = jnp.arange(B)[:, None, None, None]
    hi = jnp.arange(H)[None, None, :, None]

    def one_block(i):
        start = i * Q_BLOCK
        qb = lax.dynamic_slice_in_dim(q, start, Q_BLOCK, axis=1)
        qpos = start + jnp.arange(Q_BLOCK)
        blk = start // MOBA_BLOCK
        own_k = lax.dynamic_index_in_dim(kblk, blk, axis=1, keepdims=False)
        own_v = lax.dynamic_index_in_dim(vblk, blk, axis=1, keepdims=False)
        own_pos = blk * MOBA_BLOCK + jnp.arange(MOBA_BLOCK)
        own_mask = own_pos[None, :] <= qpos[:, None]
        sel = None
        if nbf > 0:
            idx, valid = moba_select(qb, qpos, kmean)
            sel = (kblk[bi, idx, :, hi, :], vblk[bi, idx, :, hi, :], valid)
        return moba_combine(qb, sel, own_k, own_v, own_mask)

    out = lax.map(one_block, jnp.arange(T // Q_BLOCK))
    return out.transpose(1, 0, 2, 3, 4).reshape(B, T, H, Dh)


def moba_sample(q, k_new, v_new, pool_k, pool_v, page_table, l):
    DB, DS, H, Dh = q.shape
    n_pages = page_table.shape[1]
    P = n_pages * PAGE_SIZE
    ppb = MOBA_BLOCK // PAGE_SIZE
    nbf = P // MOBA_BLOCK
    qpos = P + jnp.arange(DS)
    own_start = (P // MOBA_BLOCK) * MOBA_BLOCK
    own_pages = page_table[:, own_start // PAGE_SIZE:n_pages]
    n_own = own_pages.shape[1] * PAGE_SIZE
    own_k = jnp.concatenate([pool_k[l, own_pages].reshape(DB, n_own, H, Dh), k_new], axis=1)
    own_v = jnp.concatenate([pool_v[l, own_pages].reshape(DB, n_own, H, Dh), v_new], axis=1)
    own_pos = own_start + jnp.arange(n_own + DS)
    own_mask = own_pos[None, :] <= qpos[:, None]
    sel = None
    if nbf > 0:
        k_full = pool_k[l, page_table[:, :nbf * ppb]].reshape(DB, nbf, MOBA_BLOCK, H, Dh)
        kmean = jnp.mean(k_full.astype(jnp.float32), axis=2).astype(k_new.dtype)
        idx, valid = moba_select(q, qpos, kmean)
        KS = idx.shape[-1]
        logical = idx[..., None] * ppb + jnp.arange(ppb)
        phys = page_table[jnp.arange(DB)[:, None, None, None, None], logical]
        hi = jnp.arange(H)[None, None, :, None, None]
        sel_k = pool_k[l, phys, :, hi, :].reshape(DB, DS, H, KS, MOBA_BLOCK, Dh)
        sel_v = pool_v[l, phys, :, hi, :].reshape(DB, DS, H, KS, MOBA_BLOCK, Dh)
        sel = (sel_k, sel_v, valid)
    return moba_combine(q, sel, own_k, own_v, own_mask)


def merge_out(oa, ob, ga, gb, w_a_o, w_b_o, w_out):
    B, T = oa.shape[:2]
    ma = oa.reshape(B, T, W_A) @ w_a_o
    mb = ob.reshape(B, T, W_B) @ w_b_o
    return (jax.nn.sigmoid(ga) * ma + jax.nn.sigmoid(gb) * mb) @ w_out


def swiglu(u, w_gate, w_up, w_down):
    return (jax.nn.silu(u @ w_gate) * (u @ w_up)) @ w_down


def decoder_layer(h, c, pos, attend, w_ada, b_ada, g_n1, g_n2, w_in, b_f,
                  w_a_o, w_b_o, w_out, w_gate, w_up, w_down):
    sh1, sc1, gt1, sh2, sc2, gt2 = adaln(c, w_ada, b_ada)
    u = modulate(h, g_n1, sh1, sc1)
    qa, ka, va, qb, kb, vb, logf, ga, gb = project_in(u, pos, w_in, b_f)
    oa, ob = attend(qa, ka, va, qb, kb, vb, logf)
    h = h + gt1 * merge_out(oa, ob, ga, gb, w_a_o, w_b_o, w_out)
    h = h + gt2 * swiglu(modulate(h, g_n2, sh2, sc2), w_gate, w_up, w_down)
    return h, (ka, va, kb, vb, logf)


def setup_inputs(seed: int = 0) -> dict:
    key = jax.random.key(seed)
    ks = jax.random.split(key, 24)
    n_pages = PAST_LEN // PAGE_SIZE
    n_pool = (DEC_BATCH * n_pages * 5) // 4

    def nrm(k, shape, scale):
        return jax.random.normal(k, shape, jnp.float32) * scale

    kv_shape_a = (DEPTH, n_pool, PAGE_SIZE, H_A, HEAD_DIM)
    kv_shape_b = (DEPTH, n_pool, PAGE_SIZE, H_B, HEAD_DIM)
    page_table = jax.random.permutation(ks[10], n_pool)[:DEC_BATCH * n_pages]
    page_table = page_table.reshape(DEC_BATCH, n_pages).astype(jnp.int32)
    return {
        'x_prompt': nrm(ks[0], (BATCH, SEQ, D_MODEL), 1.0),
        'x_sample': nrm(ks[1], (DEC_BATCH, DEC_SEQ, D_MODEL), 1.0),
        'c_prompt': nrm(ks[2], (BATCH, D_MODEL), 1.0),
        'c_sample': nrm(ks[3], (DEC_BATCH, D_MODEL), 1.0),
        'cache_a_k': jax.random.normal(ks[4], kv_shape_a, jnp.float32),
        'cache_a_v': jax.random.normal(ks[5], kv_shape_a, jnp.float32),
        'cache_b_k': jax.random.normal(ks[6], kv_shape_b, jnp.float32),
        'cache_b_v': jax.random.normal(ks[7], kv_shape_b, jnp.float32),
        'cache_b_logf': jax.nn.log_sigmoid(2.0 + nrm(ks[8], (DEPTH, n_pool, PAGE_SIZE, H_B), 1.0)),
        'page_table': page_table,
        'w_ada': nrm(ks[11], (DEPTH, D_MODEL, 6 * D_MODEL), 0.5 * D_MODEL ** -0.5),
        'b_ada': nrm(ks[12], (DEPTH, 6 * D_MODEL), 0.02),
        'g_norm1': 1.0 + nrm(ks[13], (DEPTH, D_MODEL), 0.02),
        'g_norm2': 1.0 + nrm(ks[14], (DEPTH, D_MODEL), 0.02),
        'g_final': 1.0 + nrm(ks[15], (D_MODEL,), 0.02),
        'w_in': nrm(ks[16], (DEPTH, D_MODEL, N_IN), D_MODEL ** -0.5),
        'b_f': 2.0 + nrm(ks[17], (DEPTH, H_B), 0.5),
        'w_a_o': nrm(ks[18], (DEPTH, W_A, D_MODEL), W_A ** -0.5),
        'w_b_o': nrm(ks[19], (DEPTH, W_B, D_MODEL), W_B ** -0.5),
        'w_out': nrm(ks[20], (DEPTH, D_MODEL, D_MODEL), D_MODEL ** -0.5),
        'w_gate': nrm(ks[21], (DEPTH, D_MODEL, D_FF), D_MODEL ** -0.5),
        'w_up': nrm(ks[22], (DEPTH, D_MODEL, D_FF), D_MODEL ** -0.5),
        'w_down': nrm(ks[23], (DEPTH, D_FF, D_MODEL), D_FF ** -0.5),
    }


def reference(x_prompt, x_sample, c_prompt, c_sample, cache_a_k, cache_a_v, cache_b_k,
              cache_b_v, cache_b_logf, page_table, w_ada, b_ada, g_norm1, g_norm2, g_final,
              w_in, b_f, w_a_o, w_b_o, w_out, w_gate, w_up, w_down):
    pos_p = jnp.arange(SEQ)
    pos_s = PAST_LEN + jnp.arange(DEC_SEQ)
    hp, hs = x_prompt, x_sample
    new_p = [[] for _ in range(5)]
    new_s = [[] for _ in range(5)]

    def attend_prompt(qa, ka, va, qb, kb, vb, lf):
        return moba_prompt(qa, ka, va), fox_prompt(qb, kb, vb, lf)

    for l in range(DEPTH):
        lw = (w_ada[l], b_ada[l], g_norm1[l], g_norm2[l], w_in[l], b_f[l],
              w_a_o[l], w_b_o[l], w_out[l], w_gate[l], w_up[l], w_down[l])

        def attend_sample(qa, ka, va, qb, kb, vb, lf, l=l):
            oa = moba_sample(qa, ka, va, cache_a_k, cache_a_v, page_table, l)
            ob = fox_sample(qb, kb, vb, lf, cache_b_k, cache_b_v, cache_b_logf, page_table, l)
            return oa, ob

        hp, sp = decoder_layer(hp, c_prompt, pos_p, attend_prompt, *lw)
        hs, ss = decoder_layer(hs, c_sample, pos_s, attend_sample, *lw)
        for j in range(5):
            new_p[j].append(sp[j])
            new_s[j].append(ss[j])

    y_prompt = rmsnorm(hp, g_final)
    y_sample = rmsnorm(hs, g_final)
    st_p = [jnp.stack(s, axis=0) for s in new_p]
    st_s = [jnp.stack(s, axis=0) for s in new_s]
    return (y_prompt, y_sample, st_p[0], st_p[1], st_p[2], st_p[3], st_p[4],
            st_s[0], st_s[1], st_s[2], st_s[3], st_s[4])
```

```python
import functools

import numpy as np
import jax
import jax.numpy as jnp
from jax import lax
from jax.experimental import pallas as pl
from jax.experimental.pallas import tpu as pltpu

F32 = jnp.float32
BF16 = jnp.bfloat16

D_MODEL = 1024
HEAD_DIM = 64
N_HEADS = 8
BRANCH_W = N_HEADS * HEAD_DIM
D_FF = 2816
MOBA_BLOCK = 256
MOBA_TOPK = 3
ROT_DIM = 16
ROT_HALF = 8
ROPE_THETA = 500000.0
RMS_EPS = 1e-6
NEG_INF = -1e30
PAGE = 128
QK_SCALE = HEAD_DIM ** -0.5
LANES = 128
AUG_W = 128

VMEM_LIMIT = 56 * 1024 * 1024


def _cparams(sem, vmem=VMEM_LIMIT):
    return pltpu.CompilerParams(dimension_semantics=sem, vmem_limit_bytes=vmem)


def _split2(x):
    hi = x.astype(BF16)
    lo = (x - hi.astype(F32)).astype(BF16)
    return hi, lo


def _split3(x):
    hi = x.astype(BF16)
    r = x - hi.astype(F32)
    mid = r.astype(BF16)
    lo = (r - mid.astype(F32)).astype(BF16)
    return hi, mid, lo


def _dot(a, b):
    return jnp.dot(a, b, preferred_element_type=F32)


def _dot_nt(a, b):
    return lax.dot_general(a, b, (((1,), (1,)), ((), ())), preferred_element_type=F32)


def _dot3(a, b):
    ah, al = _split2(a)
    bh, bl = _split2(b)
    return _dot(ah, bh) + _dot(ah, bl) + _dot(al, bh)


def _dot3_nt(a, b):
    ah, al = _split2(a)
    bh, bl = _split2(b)
    return _dot_nt(ah, bh) + _dot_nt(ah, bl) + _dot_nt(al, bh)


def _rms(x, g):
    return x * lax.rsqrt(jnp.mean(x * x, axis=-1, keepdims=True) + RMS_EPS) * g


def _sigmoid(x):
    return 1.0 / (1.0 + jnp.exp(-x))


def _ada_kernel(c_ref, w_ref, b_ref, o_ref):
    c = c_ref[...]
    a = c * _sigmoid(c)
    o_ref[...] = _dot3(a, w_ref[...]) + b_ref[...]


def _ada(c_all, w_ada, b_ada):
    rows = c_all.shape[0]
    n = w_ada.shape[1]
    tn = 1536
    return pl.pallas_call(
        _ada_kernel,
        grid=(n // tn,),
        in_specs=[pl.BlockSpec((rows, D_MODEL), lambda j: (0, 0)),
                  pl.BlockSpec((D_MODEL, tn), lambda j: (0, j)),
                  pl.BlockSpec((1, tn), lambda j: (0, j))],
        out_specs=pl.BlockSpec((rows, tn), lambda j: (0, j)),
        out_shape=jax.ShapeDtypeStruct((rows, n), F32),
        compiler_params=_cparams(("arbitrary",)),
        name="ada",
    )(c_all, w_ada, b_ada.reshape(1, n))


def _inproj_kernel(x_ref, sh_ref, sc_ref, g_ref, wqkv_ref, wf_ref, wg_ref, bf_ref, inv_ref,
                   qa_ref, ka_ref, va_ref, qb_ref, kb_ref, vb_ref, lf_ref, sga_ref, sgb_ref,
                   *, tm, pos0, pos_stride):
    i = pl.program_id(0)
    x = x_ref[...]
    u = _rms(x, g_ref[...]) * (1.0 + sc_ref[...]) + sh_ref[...]
    ub = u.astype(BF16)

    row = lax.broadcasted_iota(jnp.int32, (tm, 1), 0) + i * tm
    pos = (pos0 + pos_stride * row).astype(F32)
    ang = pos * inv_ref[...]
    cos = jnp.cos(ang)
    sin = jnp.sin(ang)
    lane = lax.broadcasted_iota(jnp.int32, (tm, LANES), 1)
    first_half = (lane % HEAD_DIM) < ROT_HALF

    def rope(y):
        parts = []
        for g in range(BRANCH_W // LANES):
            yg = y[:, g * LANES:(g + 1) * LANES]
            partner = jnp.where(first_half,
                                -pltpu.roll(yg, LANES - ROT_HALF, axis=1),
                                pltpu.roll(yg, ROT_HALF, axis=1))
            parts.append(yg * cos + partner * sin)
        return jnp.concatenate(parts, axis=1)

    def proj(j):
        return _dot(ub, wqkv_ref[:, j * BRANCH_W:(j + 1) * BRANCH_W])

    qa_ref[...] = rope(proj(0))
    ka_ref[...] = rope(proj(1))
    va_ref[...] = proj(2)
    qb_ref[...] = proj(3)
    kb_ref[...] = proj(4)
    vb_ref[...] = proj(5)

    z = _dot(ub, wf_ref[...]) + bf_ref[...]
    lf_ref[...] = jnp.minimum(z, 0.0) - jnp.log(1.0 + jnp.exp(-jnp.abs(z)))

    sga_ref[...] = _sigmoid(_dot(ub, wg_ref[:, :D_MODEL]))
    sgb_ref[...] = _sigmoid(_dot(ub, wg_ref[:, D_MODEL:]))


def _inproj(x, sh, sc, g1, wqkv, wf, wg, bf, inv, *, tm, pos0, pos_stride):
    rows = x.shape[0]
    per_row = sh.shape[0] != 1
    mod_spec = (pl.BlockSpec((tm, D_MODEL), lambda i: (i, 0)) if per_row
                else pl.BlockSpec((1, D_MODEL), lambda i: (0, 0)))
    const = lambda shape: pl.BlockSpec(shape, lambda i: (0, 0), pipeline_mode=pl.Buffered(1))
    row_spec = lambda w: pl.BlockSpec((tm, w), lambda i: (i, 0))
    out_w = [BRANCH_W] * 6 + [LANES, D_MODEL, D_MODEL]
    return pl.pallas_call(
        functools.partial(_inproj_kernel, tm=tm, pos0=pos0, pos_stride=pos_stride),
        grid=(rows // tm,),
        in_specs=[row_spec(D_MODEL), mod_spec, mod_spec, const((1, D_MODEL)),
                  const(wqkv.shape), const(wf.shape), const(wg.shape),
                  const((1, LANES)), const((1, LANES))],
        out_specs=[row_spec(w) for w in out_w],
        out_shape=[jax.ShapeDtypeStruct((rows, w), F32) for w in out_w],
        compiler_params=_cparams(("arbitrary",)),
        name="inproj",
    )(x, sh, sc, g1, wqkv, wf, wg, bf, inv)


def _top3_mask(s, n_valid):
    rows, nblk = s.shape
    col = lax.broadcasted_iota(jnp.int32, (rows, nblk), 1).astype(F32)
    n_valid = n_valid.astype(F32)
    s = jnp.where(col < n_valid, s, NEG_INF)
    sel = jnp.zeros((rows, nblk), F32)
    for k in range(MOBA_TOPK):
        m = jnp.max(s, axis=1, keepdims=True)
        first = jnp.min(jnp.where(s == m, col, float(nblk)), axis=1, keepdims=True)
        pick = col == first
        sel = jnp.where(pick & (n_valid > k), 1.0, sel)
        s = jnp.where(pick, -3e38, s)
    return sel > 0.5


def _prep_kernel(qa_ref, ka_ref, va_ref, qb_ref, kb_ref, vb_ref, lf_ref,
                 qaug_ref, kaug_ref, v_ref, km_ref, fc_ref, *, tm, nblk):
    i = pl.program_id(0)

    @pl.when(i == 0)
    def _():
        km_ref[...] = jnp.zeros_like(km_ref)
        fc_ref[...] = jnp.zeros_like(fc_ref)

    lf = lf_ref[...]
    r_i = lax.broadcasted_iota(jnp.int32, (tm, tm), 0)
    c_i = lax.broadcasted_iota(jnp.int32, (tm, tm), 1)
    tri = jnp.where(c_i <= r_i, 1.0, 0.0).astype(BF16)
    l_hi, l_mid, l_lo = _split3(lf)
    fcum = _dot(tri, l_hi) + _dot(tri, l_mid) + _dot(tri, l_lo) + fc_ref[...]
    fc_ref[...] = fcum[tm - 1:tm, :]
    f_hi, f_mid, f_lo = [p.astype(F32) for p in _split3(-fcum)]

    qa = qa_ref[...]
    ka = ka_ref[...]
    qb = qb_ref[...]
    kb = kb_ref[...]
    km = km_ref[...]

    lane64 = lax.broadcasted_iota(jnp.int32, (tm, HEAD_DIM), 1)
    own = lane64 == i
    own_f = jnp.where(own, 1.0, 0.0)
    ones3 = jnp.where(lane64 < 3, 1.0, 0.0)

    def put(ref, g, left, right):
        ref[:, g * AUG_W:(g + 1) * AUG_W] = jnp.concatenate([left, right], axis=1).astype(BF16)

    for h in range(N_HEADS):
        hs = slice(h * HEAD_DIM, (h + 1) * HEAD_DIM)
        qh = qa[:, hs]
        s = _dot3_nt(qh, km[:, hs])
        sel = _top3_mask(s, i)
        bias = jnp.where(sel | own, 0.0, NEG_INF)
        put(qaug_ref, h, qh * QK_SCALE, bias)
        put(kaug_ref, h, ka[:, hs], own_f)

        g = N_HEADS + h
        put(qaug_ref, g, qb[:, hs] * QK_SCALE, ones3)
        dec = jnp.where(lane64 == 0, f_hi[:, h:h + 1],
                        jnp.where(lane64 == 1, f_mid[:, h:h + 1],
                                  jnp.where(lane64 == 2, f_lo[:, h:h + 1], 0.0)))
        put(kaug_ref, g, kb[:, hs], dec)

    v_ref[:, :BRANCH_W] = va_ref[...].astype(BF16)
    v_ref[:, BRANCH_W:] = vb_ref[...].astype(BF16)

    km_ref[pl.ds(i, 1), :] = jnp.mean(ka, axis=0, keepdims=True)


def _prep(qa, ka, va, qb, kb, vb, lf):
    t = qa.shape[0]
    tm = MOBA_BLOCK
    nblk = t // tm
    assert nblk <= HEAD_DIM, "block-selection bias lanes hold at most 64 blocks"
    row = lambda w: pl.BlockSpec((tm, w), lambda i: (i, 0))
    n_aug = 2 * N_HEADS * AUG_W
    return pl.pallas_call(
        functools.partial(_prep_kernel, tm=tm, nblk=nblk),
        grid=(nblk,),
        in_specs=[row(BRANCH_W)] * 6 + [row(LANES)],
        out_specs=[row(n_aug), row(n_aug), row(D_MODEL)],
        out_shape=[jax.ShapeDtypeStruct((t, n_aug), BF16),
                   jax.ShapeDtypeStruct((t, n_aug), BF16),
                   jax.ShapeDtypeStruct((t, D_MODEL), BF16)],
        scratch_shapes=[pltpu.VMEM((HEAD_DIM, BRANCH_W), F32),
                        pltpu.VMEM((1, LANES), F32)],
        compiler_params=_cparams(("arbitrary",)),
        name="prep",
    )(qa, ka, va, qb, kb, vb, lf)


def _attn_kernel(qi_ref, kj_ref, q_ref, k_ref, v_ref, o_ref, m_ref, l_ref, acc_ref, *, tq, tk):
    p_id = pl.program_id(1)
    qi = qi_ref[p_id]
    kj = kj_ref[p_id]

    @pl.when(kj == 0)
    def _():
        m_ref[...] = jnp.full_like(m_ref, -jnp.inf)
        l_ref[...] = jnp.zeros_like(l_ref)
        acc_ref[...] = jnp.zeros_like(acc_ref)

    def step(diag):
        v = v_ref[...]
        for hh in range(2):
            q = q_ref[:, hh * AUG_W:(hh + 1) * AUG_W]
            k = k_ref[:, hh * AUG_W:(hh + 1) * AUG_W]
            s = _dot_nt(q, k)
            if diag:
                r = lax.broadcasted_iota(jnp.int32, (tq, tk), 0)
                c = lax.broadcasted_iota(jnp.int32, (tq, tk), 1)
                s = jnp.where(c <= r, s, NEG_INF)
            m_prev = m_ref[hh]
            m_new = jnp.maximum(m_prev, jnp.max(s, axis=1, keepdims=True))
            alpha = jnp.exp(m_prev - m_new)
            p = jnp.exp(s - m_new)
            l_ref[hh] = alpha * l_ref[hh] + jnp.sum(p, axis=1, keepdims=True)
            acc_ref[hh] = alpha * acc_ref[hh] + _dot(p.astype(BF16), v)
            m_ref[hh] = m_new

    @pl.when(kj < qi)
    def _():
        step(False)

    @pl.when(kj == qi)
    def _():
        step(True)
        lane = lax.broadcasted_iota(jnp.int32, (tq, LANES), 1)
        o0 = acc_ref[0] / l_ref[0]
        o1 = acc_ref[1] / l_ref[1]
        o_ref[...] = jnp.where(lane < HEAD_DIM, o0, o1).astype(o_ref.dtype)


def _attn(qaug, kaug, v, *, tq):
    t = qaug.shape[0]
    tk = tq
    nq = t // tq
    pairs = [(i, j) for i in range(nq) for j in range(i + 1)]
    qi = jnp.asarray(np.array([p[0] for p in pairs], np.int32))
    kj = jnp.asarray(np.array([p[1] for p in pairs], np.int32))
    n_pair_heads = (qaug.shape[1] // AUG_W) // 2
    grid_spec = pltpu.PrefetchScalarGridSpec(
        num_scalar_prefetch=2,
        grid=(n_pair_heads, len(pairs)),
        in_specs=[pl.BlockSpec((tq, 2 * AUG_W), lambda h, p, qi, kj: (qi[p], h)),
                  pl.BlockSpec((tk, 2 * AUG_W), lambda h, p, qi, kj: (kj[p], h)),
                  pl.BlockSpec((tk, LANES), lambda h, p, qi, kj: (kj[p], h))],
        out_specs=pl.BlockSpec((tq, LANES), lambda h, p, qi, kj: (qi[p], h)),
        scratch_shapes=[pltpu.VMEM((2, tq, 1), F32),
                        pltpu.VMEM((2, tq, 1), F32),
                        pltpu.VMEM((2, tq, LANES), F32)],
    )
    return pl.pallas_call(
        functools.partial(_attn_kernel, tq=tq, tk=tk),
        grid_spec=grid_spec,
        out_shape=jax.ShapeDtypeStruct((t, v.shape[1]), BF16),
        compiler_params=_cparams(("arbitrary", "arbitrary")),
        name="attn",
    )(qi, kj, qaug, kaug, v)


def _post_kernel(x_ref, o_ref, sga_ref, sgb_ref, gt1_ref, sh2_ref, sc2_ref, gt2_ref,
                 g2_ref, gf_ref, wao_ref, wbo_ref, wout_ref, wgate_ref, wup_ref, wdown_ref,
                 y_ref):
    o = o_ref[...].astype(BF16)
    ma = _dot(o[:, :BRANCH_W], wao_ref[...])
    mb = _dot(o[:, BRANCH_W:], wbo_ref[...])
    mix = (sga_ref[...] * ma + sgb_ref[...] * mb).astype(BF16)
    h1 = x_ref[...] + gt1_ref[...] * _dot(mix, wout_ref[...])
    u2 = (_rms(h1, g2_ref[...]) * (1.0 + sc2_ref[...]) + sh2_ref[...]).astype(BF16)
    gate = _dot(u2, wgate_ref[...])
    up = _dot(u2, wup_ref[...])
    act = (gate * _sigmoid(gate) * up).astype(BF16)
    h2 = h1 + gt2_ref[...] * _dot(act, wdown_ref[...])
    y_ref[...] = _rms(h2, gf_ref[...])


def _post(x, o, sga, sgb, gt1, sh2, sc2, gt2, g2, gf, wao, wbo, wout, wgate, wup, wdown, *, tm):
    rows = x.shape[0]
    per_row = gt1.shape[0] != 1
    mod_spec = (pl.BlockSpec((tm, D_MODEL), lambda i: (i, 0)) if per_row
                else pl.BlockSpec((1, D_MODEL), lambda i: (0, 0)))
    const = lambda a: pl.BlockSpec(a.shape, lambda i: (0, 0), pipeline_mode=pl.Buffered(1))
    row = pl.BlockSpec((tm, D_MODEL), lambda i: (i, 0))
    return pl.pallas_call(
        _post_kernel,
        grid=(rows // tm,),
        in_specs=[row, row, row, row, mod_spec, mod_spec, mod_spec, mod_spec,
                  const(g2), const(gf), const(wao), const(wbo), const(wout),
                  const(wgate), const(wup), const(wdown)],
        out_specs=row,
        out_shape=jax.ShapeDtypeStruct((rows, D_MODEL), F32),
        compiler_params=_cparams(("arbitrary",)),
        name="post",
    )(x, o, sga, sgb, gt1, sh2, sc2, gt2, g2, gf, wao, wbo, wout, wgate, wup, wdown)


def _head_mask(rows=N_HEADS, width=BRANCH_W):
    r = lax.broadcasted_iota(jnp.int32, (rows, width), 0)
    c = lax.broadcasted_iota(jnp.int32, (rows, width), 1)
    return (c // HEAD_DIM) == r


def _fox_dec_kernel(pt_ref, q_ref, kn_ref, vn_ref, lfn_ref, *refs, npg):
    k_refs = refs[:npg]
    v_refs = refs[npg:2 * npg]
    lf_refs = refs[2 * npg:3 * npg]
    o_ref = refs[3 * npg]
    m_ref, l_ref, acc_ref, car_ref = refs[3 * npg + 1:]
    c = pl.program_id(1)
    nc = pl.num_programs(1)

    hm = _head_mask()
    qbd = jnp.where(hm, q_ref[0] * QK_SCALE, 0.0)

    @pl.when(c == 0)
    def _():
        m_ref[...] = jnp.sum(qbd * kn_ref[0], axis=1, keepdims=True)
        l_ref[...] = jnp.ones_like(l_ref)
        acc_ref[...] = jnp.broadcast_to(vn_ref[0], acc_ref.shape)
        car_ref[...] = jnp.zeros_like(car_ref)

    qb16 = qbd.astype(BF16)
    r_i = lax.broadcasted_iota(jnp.int32, (PAGE, PAGE), 0)
    c_i = lax.broadcasted_iota(jnp.int32, (PAGE, PAGE), 1)
    later = jnp.where(r_i > c_i, 1.0, 0.0).astype(BF16)
    lfn = lfn_ref[0]

    carry = car_ref[...]
    scores = []
    for i in range(npg):
        lft = lf_refs[i][0]
        h3, m3, l3 = _split3(lft)
        suffix = _dot(h3, later) + _dot(m3, later) + _dot(l3, later) + carry
        carry = carry + jnp.sum(lft, axis=1, keepdims=True)
        kt = k_refs[i][0].reshape(BRANCH_W, PAGE).astype(BF16)
        s = _dot(qb16, kt)
        scores.append(s + suffix + lfn)
    car_ref[...] = carry

    s_all = jnp.concatenate(scores, axis=1)
    m_prev = m_ref[...]
    m_new = jnp.maximum(m_prev, jnp.max(s_all, axis=1, keepdims=True))
    alpha = jnp.exp(m_prev - m_new)
    p_all = jnp.exp(s_all - m_new)
    l_ref[...] = alpha * l_ref[...] + jnp.sum(p_all, axis=1, keepdims=True)
    acc = alpha * acc_ref[...]
    for i in range(npg):
        p = p_all[:, i * PAGE:(i + 1) * PAGE].astype(BF16)
        vt = v_refs[i][0].reshape(BRANCH_W, PAGE).astype(BF16)
        acc = acc + _dot_nt(p, vt)
    acc_ref[...] = acc
    m_ref[...] = m_new

    @pl.when(c == nc - 1)
    def _():
        o = jnp.where(hm, acc / l_ref[...], 0.0)
        o_ref[0] = jnp.sum(o, axis=0, keepdims=True)


def _fox_dec(page_table, q, k_new, v_new, lf_new, pool_k, pool_v, pool_lft, *, npg):
    nseq, n_pages = page_table.shape
    nchunk = n_pages // npg
    last = n_pages - 1
    pt_flat = page_table.reshape(-1)

    def page_map(i, nd):
        return lambda b, c, pt: (pt[b * n_pages + (last - (c * npg + i))],) + (0,) * (nd - 1)

    seq3 = lambda w: pl.BlockSpec((1, 1, w), lambda b, c, pt: (b, 0, 0))
    page_blk = (1, N_HEADS, HEAD_DIM, PAGE)
    in_specs = [seq3(BRANCH_W), seq3(BRANCH_W), seq3(BRANCH_W),
                pl.BlockSpec((1, N_HEADS, 1), lambda b, c, pt: (b, 0, 0))]
    in_specs += [pl.BlockSpec(page_blk, page_map(i, 4)) for i in range(npg)]
    in_specs += [pl.BlockSpec(page_blk, page_map(i, 4)) for i in range(npg)]
    in_specs += [pl.BlockSpec((1, N_HEADS, PAGE), page_map(i, 3)) for i in range(npg)]
    grid_spec = pltpu.PrefetchScalarGridSpec(
        num_scalar_prefetch=1,
        grid=(nseq, nchunk),
        in_specs=in_specs,
        out_specs=seq3(BRANCH_W),
        scratch_shapes=[pltpu.VMEM((N_HEADS, 1), F32), pltpu.VMEM((N_HEADS, 1), F32),
                        pltpu.VMEM((N_HEADS, BRANCH_W), F32), pltpu.VMEM((N_HEADS, 1), F32)],
    )
    out = pl.pallas_call(
        functools.partial(_fox_dec_kernel, npg=npg),
        grid_spec=grid_spec,
        out_shape=jax.ShapeDtypeStruct((nseq, 1, BRANCH_W), F32),
        compiler_params=_cparams(("arbitrary", "arbitrary")),
        name="fox_dec",
    )(pt_flat, q.reshape(nseq, 1, BRANCH_W), k_new.reshape(nseq, 1, BRANCH_W),
      v_new.reshape(nseq, 1, BRANCH_W), lf_new.reshape(nseq, N_HEADS, 1),
      *([pool_k] * npg), *([pool_v] * npg), *([pool_lft] * npg))
    return out.reshape(nseq, BRANCH_W)


def _moba_sel_kernel(pt_ref, q_ref, *refs, npg, nblk):
    k_refs = refs[:npg]
    idx_ref = refs[npg]
    km_ref = refs[npg + 1]
    c = pl.program_id(1)
    nc = pl.num_programs(1)
    ppb = MOBA_BLOCK // PAGE

    @pl.when(c == 0)
    def _():
        km_ref[...] = jnp.zeros_like(km_ref)

    lane = lax.broadcasted_iota(jnp.int32, (BRANCH_W, LANES), 1)
    km = km_ref[...]
    for n in range(npg // ppb):
        tot = k_refs[n * ppb][0].reshape(BRANCH_W, PAGE)
        for j in range(1, ppb):
            tot = tot + k_refs[n * ppb + j][0].reshape(BRANCH_W, PAGE)
        mean = jnp.sum(tot, axis=1, keepdims=True) * (1.0 / MOBA_BLOCK)
        km = jnp.where(lane == c * (npg // ppb) + n, mean, km)
    km_ref[...] = km

    @pl.when(c == nc - 1)
    def _():
        qbd = jnp.where(_head_mask(), q_ref[0], 0.0)
        s = _dot3(qbd, km)
        col = lax.broadcasted_iota(jnp.int32, s.shape, 1).astype(F32)
        s = jnp.where(col < nblk, s, NEG_INF)
        out = jnp.zeros(s.shape, F32)
        for k in range(MOBA_TOPK):
            m = jnp.max(s, axis=1, keepdims=True)
            first = jnp.min(jnp.where(s == m, col, float(LANES)), axis=1, keepdims=True)
            out = jnp.where(col == k, first, out)
            s = jnp.where(col == first, -3e38, s)
        idx_ref[0] = out.astype(jnp.int32)


def _moba_sel(page_table, q, pool_k, *, npg):
    nseq, n_pages = page_table.shape
    nblk = n_pages * PAGE // MOBA_BLOCK
    assert nblk >= MOBA_TOPK and nblk <= LANES
    pt_flat = page_table.reshape(-1)

    def page_map(i):
        return lambda b, c, pt: (pt[b * n_pages + c * npg + i], 0, 0, 0)

    grid_spec = pltpu.PrefetchScalarGridSpec(
        num_scalar_prefetch=1,
        grid=(nseq, n_pages // npg),
        in_specs=[pl.BlockSpec((1, 1, BRANCH_W), lambda b, c, pt: (b, 0, 0))]
        + [pl.BlockSpec((1, N_HEADS, HEAD_DIM, PAGE), page_map(i)) for i in range(npg)],
        out_specs=pl.BlockSpec((1, N_HEADS, LANES), lambda b, c, pt: (b, 0, 0)),
        scratch_shapes=[pltpu.VMEM((BRANCH_W, LANES), F32)],
    )
    return pl.pallas_call(
        functools.partial(_moba_sel_kernel, npg=npg, nblk=nblk),
        grid_spec=grid_spec,
        out_shape=jax.ShapeDtypeStruct((nseq, N_HEADS, LANES), jnp.int32),
        compiler_params=_cparams(("arbitrary", "arbitrary")),
        name="moba_sel",
    )(pt_flat, q.reshape(nseq, 1, BRANCH_W), *([pool_k] * npg))


def _moba_dec_kernel(pt_ref, idx_ref, q_ref, kn_ref, vn_ref, *refs, nsel):
    k_refs = refs[:nsel]
    v_refs = refs[nsel:2 * nsel]
    o_ref = refs[2 * nsel]
    q = q_ref[0] * QK_SCALE
    q8 = jnp.broadcast_to(q, (8, HEAD_DIM)).astype(BF16)
    s_new = jnp.sum(q * kn_ref[0], axis=1, keepdims=True)
    scores = [_dot(q8, k_refs[i][0, 0].astype(BF16)) for i in range(nsel)]
    s_all = jnp.concatenate(scores, axis=1)
    m = jnp.maximum(jnp.max(s_all, axis=1, keepdims=True), s_new)
    p_all = jnp.exp(s_all - m)
    p_new = jnp.exp(s_new - m)
    l = jnp.sum(p_all, axis=1, keepdims=True) + p_new
    acc = p_new * vn_ref[0]
    for i in range(nsel):
        p = p_all[:, i * PAGE:(i + 1) * PAGE].astype(BF16)
        acc = acc + _dot_nt(p, v_refs[i][0, 0].astype(BF16))
    o_ref[0] = (acc / l)[0:1, :]


def _moba_dec(page_table, idx, q, k_new, v_new, pool_k, pool_v):
    nseq, n_pages = page_table.shape
    ppb = MOBA_BLOCK // PAGE
    nsel = MOBA_TOPK * ppb
    pt_flat = page_table.reshape(-1)
    idx_flat = idx[:, :, :MOBA_TOPK].reshape(-1)

    def page_map(k, j):
        def f(b, h, pt, ix):
            blk = ix[(b * N_HEADS + h) * MOBA_TOPK + k]
            return (pt[b * n_pages + blk * ppb + j], h, 0, 0)
        return f

    head_spec = pl.BlockSpec((1, 1, HEAD_DIM), lambda b, h, pt, ix: (b * N_HEADS + h, 0, 0))
    page_blk = (1, 1, HEAD_DIM, PAGE)
    sel = [(k, j) for k in range(MOBA_TOPK) for j in range(ppb)]
    grid_spec = pltpu.PrefetchScalarGridSpec(
        num_scalar_prefetch=2,
        grid=(nseq, N_HEADS),
        in_specs=[head_spec, head_spec, head_spec]
        + [pl.BlockSpec(page_blk, page_map(k, j)) for k, j in sel]
        + [pl.BlockSpec(page_blk, page_map(k, j)) for k, j in sel],
        out_specs=head_spec,
    )
    per_head = lambda a: a.reshape(nseq * N_HEADS, 1, HEAD_DIM)
    out = pl.pallas_call(
        functools.partial(_moba_dec_kernel, nsel=nsel),
        grid_spec=grid_spec,
        out_shape=jax.ShapeDtypeStruct((nseq * N_HEADS, 1, HEAD_DIM), F32),
        compiler_params=_cparams(("arbitrary", "arbitrary")),
        name="moba_dec",
    )(pt_flat, idx_flat, per_head(q), per_head(k_new), per_head(v_new),
      *([pool_k] * nsel), *([pool_v] * nsel))
    return out.reshape(nseq, BRANCH_W)


def kernel(x_prompt, x_sample, c_prompt, c_sample, cache_a_k, cache_a_v, cache_b_k, cache_b_v,
           cache_b_logf, page_table, w_ada, b_ada, g_norm1, g_norm2, g_final, w_in, b_f,
           w_a_o, w_b_o, w_out, w_gate, w_up, w_down):
    depth = w_ada.shape[0]
    assert depth == 1
    batch, seq, _ = x_prompt.shape
    nseq, dec_seq, _ = x_sample.shape
    assert batch == 1 and dec_seq == 1
    n_pool = cache_a_k.shape[1]
    n_pages = page_table.shape[1]
    past_len = n_pages * PAGE
    assert past_len % MOBA_BLOCK == 0

    w_in0 = w_in[0]
    n_qkv = 6 * BRANCH_W
    wqkv = w_in0[:, :n_qkv].astype(BF16)
    wf = jnp.pad(w_in0[:, n_qkv:n_qkv + N_HEADS], ((0, 0), (0, LANES - N_HEADS))).astype(BF16)
    wg = w_in0[:, n_qkv + N_HEADS:].astype(BF16)
    bf = jnp.pad(b_f[0], (0, LANES - N_HEADS)).reshape(1, LANES)
    wao = w_a_o[0].astype(BF16)
    wbo = w_b_o[0].astype(BF16)
    wout = w_out[0].astype(BF16)
    wgate = w_gate[0].astype(BF16)
    wup = w_up[0].astype(BF16)
    wdown = w_down[0].astype(BF16)
    g1 = g_norm1[0].reshape(1, D_MODEL)
    g2 = g_norm2[0].reshape(1, D_MODEL)
    gf = g_final.reshape(1, D_MODEL)
    inv8 = ROPE_THETA ** (-jnp.arange(ROT_HALF, dtype=F32) / ROT_HALF)
    lane = np.arange(LANES)
    inv = jnp.where(jnp.asarray((lane % HEAD_DIM) < ROT_DIM), inv8[lane % ROT_HALF], 0.0)
    inv = inv.reshape(1, LANES).astype(F32)

    pad = (-(batch + nseq)) % 8
    c_all = jnp.concatenate([c_prompt, c_sample, jnp.zeros((pad, D_MODEL), F32)], axis=0)
    mods = _ada(c_all, w_ada[0], b_ada[0])
    mp = [mods[0:1, j * D_MODEL:(j + 1) * D_MODEL] for j in range(6)]
    ms = [mods[batch:batch + nseq, j * D_MODEL:(j + 1) * D_MODEL] for j in range(6)]

    xp = x_prompt.reshape(seq, D_MODEL)
    qa, ka, va, qb, kb, vb, lf, sga, sgb = _inproj(
        xp, mp[0], mp[1], g1, wqkv, wf, wg, bf, inv, tm=256, pos0=0, pos_stride=1)
    qaug, kaug, vcat = _prep(qa, ka, va, qb, kb, vb, lf)
    o_p = _attn(qaug, kaug, vcat, tq=512)
    y_p = _post(xp, o_p, sga, sgb, mp[2], mp[3], mp[4], mp[5], g2, gf,
                wao, wbo, wout, wgate, wup, wdown, tm=256)

    xs = x_sample.reshape(nseq, D_MODEL)
    qa_s, ka_s, va_s, qb_s, kb_s, vb_s, lf_s, sga_s, sgb_s = _inproj(
        xs, ms[0], ms[1], g1, wqkv, wf, wg, bf, inv, tm=nseq, pos0=past_len, pos_stride=0)
    pool_t = lambda c: jnp.transpose(c[0], (0, 2, 3, 1))
    pool_ak, pool_av = pool_t(cache_a_k), pool_t(cache_a_v)
    pool_bk, pool_bv = pool_t(cache_b_k), pool_t(cache_b_v)
    pool_lft = jnp.transpose(cache_b_logf[0], (0, 2, 1))
    ob_s = _fox_dec(page_table, qb_s, kb_s, vb_s, lf_s[:, :N_HEADS],
                    pool_bk, pool_bv, pool_lft, npg=8)
    idx = _moba_sel(page_table, qa_s, pool_ak, npg=16)
    oa_s = _moba_dec(page_table, idx, qa_s, ka_s, va_s, pool_ak, pool_av)
    o_s = jnp.concatenate([oa_s, ob_s], axis=1)
    y_s = _post(xs, o_s, sga_s, sgb_s, ms[2], ms[3], ms[4], ms[5], g2, gf,
                wao, wbo, wout, wgate, wup, wdown, tm=nseq)

    return (y_p.reshape(batch, seq, D_MODEL),
            y_s.reshape(nseq, dec_seq, D_MODEL),
            ka.reshape(depth, batch, seq, N_HEADS, HEAD_DIM),
            va.reshape(depth, batch, seq, N_HEADS, HEAD_DIM),
            kb.reshape(depth, batch, seq, N_HEADS, HEAD_DIM),
            vb.reshape(depth, batch, seq, N_HEADS, HEAD_DIM),
            lf[:, :N_HEADS].reshape(depth, batch, seq, N_HEADS),
            ka_s.reshape(depth, nseq, dec_seq, N_HEADS, HEAD_DIM),
            va_s.reshape(depth, nseq, dec_seq, N_HEADS, HEAD_DIM),
            kb_s.reshape(depth, nseq, dec_seq, N_HEADS, HEAD_DIM),
            vb_s.reshape(depth, nseq, dec_seq, N_HEADS, HEAD_DIM),
            lf_s[:, :N_HEADS].reshape(depth, nseq, dec_seq, N_HEADS))
```

```python
import functools

import numpy as np
import jax
import jax.numpy as jnp
from jax import lax
from jax.experimental import pallas as pl
from jax.experimental.pallas import tpu as pltpu

F32 = jnp.float32
BF16 = jnp.bfloat16

D_MODEL = 1024
HEAD_DIM = 64
N_HEADS = 8
BRANCH_W = N_HEADS * HEAD_DIM
D_FF = 2816
MOBA_BLOCK = 256
MOBA_TOPK = 3
ROT_DIM = 16
ROT_HALF = 8
ROPE_THETA = 500000.0
RMS_EPS = 1e-6
NEG_INF = -1e30
PAGE = 128
QK_SCALE = HEAD_DIM ** -0.5
LOG2E = 1.4426950408889634
LANES = 128
AUG_W = 128

VMEM_LIMIT = 56 * 1024 * 1024


def _cparams(sem, vmem=VMEM_LIMIT):
    return pltpu.CompilerParams(dimension_semantics=sem, vmem_limit_bytes=vmem)


def _split2(x):
    hi = x.astype(BF16)
    lo = (x - hi.astype(F32)).astype(BF16)
    return hi, lo


def _split3(x):
    hi = x.astype(BF16)
    r = x - hi.astype(F32)
    mid = r.astype(BF16)
    lo = (r - mid.astype(F32)).astype(BF16)
    return hi, mid, lo


def _dot(a, b):
    return jnp.dot(a, b, preferred_element_type=F32)


def _dot_nt(a, b):
    return lax.dot_general(a, b, (((1,), (1,)), ((), ())), preferred_element_type=F32)


def _dot3(a, b):
    ah, al = _split2(a)
    bh, bl = _split2(b)
    return _dot(ah, bh) + _dot(ah, bl) + _dot(al, bh)


def _dot3_nt(a, b):
    ah, al = _split2(a)
    bh, bl = _split2(b)
    return _dot_nt(ah, bh) + _dot_nt(ah, bl) + _dot_nt(al, bh)


def _rms(x, g):
    return x * lax.rsqrt(jnp.mean(x * x, axis=-1, keepdims=True) + RMS_EPS) * g


def _sigmoid(x):
    return 1.0 / (1.0 + jnp.exp(-x))


def _ada_kernel(c_ref, w_ref, b_ref, o_ref):
    c = c_ref[...]
    a = c * _sigmoid(c)
    o_ref[...] = _dot3(a, w_ref[...]) + b_ref[...]


def _ada(c_all, w_ada, b_ada):
    rows = c_all.shape[0]
    n = w_ada.shape[1]
    tn = 1536
    return pl.pallas_call(
        _ada_kernel,
        grid=(n // tn,),
        in_specs=[pl.BlockSpec((rows, D_MODEL), lambda j: (0, 0)),
                  pl.BlockSpec((D_MODEL, tn), lambda j: (0, j)),
                  pl.BlockSpec((1, tn), lambda j: (0, j))],
        out_specs=pl.BlockSpec((rows, tn), lambda j: (0, j)),
        out_shape=jax.ShapeDtypeStruct((rows, n), F32),
        compiler_params=_cparams(("arbitrary",)),
        name="ada",
    )(c_all, w_ada, b_ada.reshape(1, n))


def _inproj_kernel(x_ref, sh_ref, sc_ref, g_ref, wqkv_ref, wf_ref, wg_ref, bf_ref, inv_ref,
                   qa_ref, ka_ref, va_ref, qb_ref, kb_ref, vb_ref, lf_ref, sga_ref, sgb_ref,
                   *, tm, pos0, pos_stride):
    i = pl.program_id(0)
    x = x_ref[...]
    u = _rms(x, g_ref[...]) * (1.0 + sc_ref[...]) + sh_ref[...]
    ub = u.astype(BF16)

    row = lax.broadcasted_iota(jnp.int32, (tm, 1), 0) + i * tm
    pos = (pos0 + pos_stride * row).astype(F32)
    ang = pos * inv_ref[...]
    cos = jnp.cos(ang)
    sin = jnp.sin(ang)
    lane = lax.broadcasted_iota(jnp.int32, (tm, LANES), 1)
    first_half = (lane % HEAD_DIM) < ROT_HALF

    def rope(y):
        parts = []
        for g in range(BRANCH_W // LANES):
            yg = y[:, g * LANES:(g + 1) * LANES]
            partner = jnp.where(first_half,
                                -pltpu.roll(yg, LANES - ROT_HALF, axis=1),
                                pltpu.roll(yg, ROT_HALF, axis=1))
            parts.append(yg * cos + partner * sin)
        return jnp.concatenate(parts, axis=1)

    def proj(j):
        return _dot(ub, wqkv_ref[:, j * BRANCH_W:(j + 1) * BRANCH_W])

    qa_ref[...] = rope(proj(0))
    ka_ref[...] = rope(proj(1))
    va_ref[...] = proj(2)
    qb_ref[...] = proj(3)
    kb_ref[...] = proj(4)
    vb_ref[...] = proj(5)

    z = _dot(ub, wf_ref[...]) + bf_ref[...]
    lf_ref[...] = jnp.minimum(z, 0.0) - jnp.log(1.0 + jnp.exp(-jnp.abs(z)))

    sga_ref[...] = _sigmoid(_dot(ub, wg_ref[:, :D_MODEL]))
    sgb_ref[...] = _sigmoid(_dot(ub, wg_ref[:, D_MODEL:]))


def _inproj(x, sh, sc, g1, wqkv, wf, wg, bf, inv, *, tm, pos0, pos_stride):
    rows = x.shape[0]
    per_row = sh.shape[0] != 1
    mod_spec = (pl.BlockSpec((tm, D_MODEL), lambda i: (i, 0)) if per_row
                else pl.BlockSpec((1, D_MODEL), lambda i: (0, 0)))
    const = lambda shape: pl.BlockSpec(shape, lambda i: (0, 0), pipeline_mode=pl.Buffered(1))
    row_spec = lambda w: pl.BlockSpec((tm, w), lambda i: (i, 0))
    out_w = [BRANCH_W] * 6 + [LANES, D_MODEL, D_MODEL]
    return pl.pallas_call(
        functools.partial(_inproj_kernel, tm=tm, pos0=pos0, pos_stride=pos_stride),
        grid=(rows // tm,),
        in_specs=[row_spec(D_MODEL), mod_spec, mod_spec, const((1, D_MODEL)),
                  const(wqkv.shape), const(wf.shape), const(wg.shape),
                  const((1, LANES)), const((1, LANES))],
        out_specs=[row_spec(w) for w in out_w],
        out_shape=[jax.ShapeDtypeStruct((rows, w), F32) for w in out_w],
        compiler_params=_cparams(("arbitrary",)),
        name="inproj",
    )(x, sh, sc, g1, wqkv, wf, wg, bf, inv)


def _top3_mask(s, n_valid):
    rows, nblk = s.shape
    col = lax.broadcasted_iota(jnp.int32, (rows, nblk), 1).astype(F32)
    n_valid = n_valid.astype(F32)
    s = jnp.where(col < n_valid, s, NEG_INF)
    sel = jnp.zeros((rows, nblk), F32)
    for k in range(MOBA_TOPK):
        m = jnp.max(s, axis=1, keepdims=True)
        first = jnp.min(jnp.where(s == m, col, float(nblk)), axis=1, keepdims=True)
        pick = col == first
        sel = jnp.where(pick & (n_valid > k), 1.0, sel)
        s = jnp.where(pick, -3e38, s)
    return sel > 0.5


def _prep_kernel(qa_ref, ka_ref, va_ref, qb_ref, kb_ref, vb_ref, lf_ref,
                 qaug_ref, kaug_ref, vaug_ref, km_ref, fc_ref, *, tm, nblk):
    i = pl.program_id(0)

    @pl.when(i == 0)
    def _():
        km_ref[...] = jnp.zeros_like(km_ref)
        fc_ref[...] = jnp.zeros_like(fc_ref)

    lf = lf_ref[...]
    r_i = lax.broadcasted_iota(jnp.int32, (tm, tm), 0)
    c_i = lax.broadcasted_iota(jnp.int32, (tm, tm), 1)
    tri = jnp.where(c_i <= r_i, 1.0, 0.0).astype(BF16)
    l_hi, l_mid, l_lo = _split3(lf)
    fcum = _dot(tri, l_hi) + _dot(tri, l_mid) + _dot(tri, l_lo) + fc_ref[...]
    fc_ref[...] = fcum[tm - 1:tm, :]
    f_hi, f_mid, f_lo = [p.astype(F32) for p in _split3(-fcum * LOG2E)]

    qa = qa_ref[...]
    ka = ka_ref[...]
    va = va_ref[...]
    qb = qb_ref[...]
    kb = kb_ref[...]
    vb = vb_ref[...]
    km = km_ref[...]

    lane64 = lax.broadcasted_iota(jnp.int32, (tm, HEAD_DIM), 1)
    own = lane64 == i
    own_f = jnp.where(own, 1.0, 0.0)
    ones3 = jnp.where(lane64 < 3, 1.0, 0.0)
    ones1 = jnp.where(lane64 < 1, 1.0, 0.0)

    def put(ref, g, left, right):
        ref[:, g * AUG_W:(g + 1) * AUG_W] = jnp.concatenate([left, right], axis=1).astype(BF16)

    for h in range(N_HEADS):
        hs = slice(h * HEAD_DIM, (h + 1) * HEAD_DIM)
        qh = qa[:, hs]
        s = _dot3_nt(qh, km[:, hs])
        sel = _top3_mask(s, i)
        bias = jnp.where(sel | own, 0.0, NEG_INF)
        put(qaug_ref, h, qh * (QK_SCALE * LOG2E), bias)
        put(kaug_ref, h, ka[:, hs], own_f)
        put(vaug_ref, h, va[:, hs], ones1)

        g = N_HEADS + h
        put(qaug_ref, g, qb[:, hs] * (QK_SCALE * LOG2E), ones3)
        dec = jnp.where(lane64 == 0, f_hi[:, h:h + 1],
                        jnp.where(lane64 == 1, f_mid[:, h:h + 1],
                                  jnp.where(lane64 == 2, f_lo[:, h:h + 1], 0.0)))
        put(kaug_ref, g, kb[:, hs], dec)
        put(vaug_ref, g, vb[:, hs], ones1)

    km_ref[pl.ds(i, 1), :] = jnp.mean(ka, axis=0, keepdims=True)


def _prep(qa, ka, va, qb, kb, vb, lf):
    t = qa.shape[0]
    tm = MOBA_BLOCK
    nblk = t // tm
    assert nblk <= HEAD_DIM, "block-selection bias lanes hold at most 64 blocks"
    row = lambda w: pl.BlockSpec((tm, w), lambda i: (i, 0))
    n_aug = 2 * N_HEADS * AUG_W
    return pl.pallas_call(
        functools.partial(_prep_kernel, tm=tm, nblk=nblk),
        grid=(nblk,),
        in_specs=[row(BRANCH_W)] * 6 + [row(LANES)],
        out_specs=[row(n_aug)] * 3,
        out_shape=[jax.ShapeDtypeStruct((t, n_aug), BF16)] * 3,
        scratch_shapes=[pltpu.VMEM((HEAD_DIM, BRANCH_W), F32),
                        pltpu.VMEM((1, LANES), F32)],
        compiler_params=_cparams(("arbitrary",)),
        name="prep",
    )(qa, ka, va, qb, kb, vb, lf)


def _attn_kernel(qi_ref, kj_ref, q_ref, k_ref, v_ref, o_ref, m_ref, acc_ref, *, tq, tk):
    p_id = pl.program_id(1)
    qi = qi_ref[p_id]
    kj = kj_ref[p_id]

    @pl.when(kj == 0)
    def _():
        m_ref[...] = jnp.full_like(m_ref, -jnp.inf)
        acc_ref[...] = jnp.zeros_like(acc_ref)

    def step(diag):
        for hh in range(2):
            hs = slice(hh * AUG_W, (hh + 1) * AUG_W)
            s = _dot_nt(q_ref[:, hs], k_ref[:, hs])
            if diag:
                r = lax.broadcasted_iota(jnp.int32, (tq, tk), 0)
                c = lax.broadcasted_iota(jnp.int32, (tq, tk), 1)
                s = jnp.where(c <= r, s, NEG_INF)
            m_prev = m_ref[hh]
            m_new = jnp.maximum(m_prev, jnp.max(s, axis=1, keepdims=True))
            alpha = jnp.exp2(m_prev - m_new)
            p = jnp.concatenate(
                [jnp.exp2(s[:, j * LANES:(j + 1) * LANES] - m_new).astype(BF16)
                 for j in range(tk // LANES)], axis=1)
            acc_ref[hh] = alpha * acc_ref[hh] + _dot(p, v_ref[:, hs])
            m_ref[hh] = m_new

    @pl.when(kj < qi)
    def _():
        step(False)

    @pl.when(kj == qi)
    def _():
        step(True)
        outs = []
        for hh in range(2):
            acc = acc_ref[hh]
            outs.append(acc[:, :HEAD_DIM] / acc[:, HEAD_DIM:HEAD_DIM + 1])
        o_ref[...] = jnp.concatenate(outs, axis=1).astype(o_ref.dtype)


def _attn(qaug, kaug, v, *, tq):
    t = qaug.shape[0]
    tk = tq
    nq = t // tq
    pairs = [(i, j) for i in range(nq) for j in range(i + 1)]
    qi = jnp.asarray(np.array([p[0] for p in pairs], np.int32))
    kj = jnp.asarray(np.array([p[1] for p in pairs], np.int32))
    n_pair_heads = (qaug.shape[1] // AUG_W) // 2
    grid_spec = pltpu.PrefetchScalarGridSpec(
        num_scalar_prefetch=2,
        grid=(n_pair_heads, len(pairs)),
        in_specs=[pl.BlockSpec((tq, 2 * AUG_W), lambda h, p, qi, kj: (qi[p], h)),
                  pl.BlockSpec((tk, 2 * AUG_W), lambda h, p, qi, kj: (kj[p], h)),
                  pl.BlockSpec((tk, 2 * AUG_W), lambda h, p, qi, kj: (kj[p], h))],
        out_specs=pl.BlockSpec((tq, LANES), lambda h, p, qi, kj: (qi[p], h)),
        scratch_shapes=[pltpu.VMEM((2, tq, LANES), F32),
                        pltpu.VMEM((2, tq, LANES), F32)],
    )
    return pl.pallas_call(
        functools.partial(_attn_kernel, tq=tq, tk=tk),
        grid_spec=grid_spec,
        out_shape=jax.ShapeDtypeStruct((t, n_pair_heads * LANES), BF16),
        compiler_params=_cparams(("arbitrary", "arbitrary")),
        name="attn",
    )(qi, kj, qaug, kaug, v)


def _post_kernel(x_ref, o_ref, sga_ref, sgb_ref, gt1_ref, sh2_ref, sc2_ref, gt2_ref,
                 g2_ref, gf_ref, wao_ref, wbo_ref, wout_ref, wgate_ref, wup_ref, wdown_ref,
                 y_ref):
    o = o_ref[...].astype(BF16)
    ma = _dot(o[:, :BRANCH_W], wao_ref[...])
    mb = _dot(o[:, BRANCH_W:], wbo_ref[...])
    mix = (sga_ref[...] * ma + sgb_ref[...] * mb).astype(BF16)
    h1 = x_ref[...] + gt1_ref[...] * _dot(mix, wout_ref[...])
    u2 = (_rms(h1, g2_ref[...]) * (1.0 + sc2_ref[...]) + sh2_ref[...]).astype(BF16)
    gate = _dot(u2, wgate_ref[...])
    up = _dot(u2, wup_ref[...])
    act = (gate * _sigmoid(gate) * up).astype(BF16)
    h2 = h1 + gt2_ref[...] * _dot(act, wdown_ref[...])
    y_ref[...] = _rms(h2, gf_ref[...])


def _post(x, o, sga, sgb, gt1, sh2, sc2, gt2, g2, gf, wao, wbo, wout, wgate, wup, wdown, *, tm):
    rows = x.shape[0]
    per_row = gt1.shape[0] != 1
    mod_spec = (pl.BlockSpec((tm, D_MODEL), lambda i: (i, 0)) if per_row
                else pl.BlockSpec((1, D_MODEL), lambda i: (0, 0)))
    const = lambda a: pl.BlockSpec(a.shape, lambda i: (0, 0), pipeline_mode=pl.Buffered(1))
    row = pl.BlockSpec((tm, D_MODEL), lambda i: (i, 0))
    return pl.pallas_call(
        _post_kernel,
        grid=(rows // tm,),
        in_specs=[row, row, row, row, mod_spec, mod_spec, mod_spec, mod_spec,
                  const(g2), const(gf), const(wao), const(wbo), const(wout),
                  const(wgate), const(wup), const(wdown)],
        out_specs=row,
        out_shape=jax.ShapeDtypeStruct((rows, D_MODEL), F32),
        compiler_params=_cparams(("arbitrary",)),
        name="post",
    )(x, o, sga, sgb, gt1, sh2, sc2, gt2, g2, gf, wao, wbo, wout, wgate, wup, wdown)


def _head_mask(rows=N_HEADS, width=BRANCH_W):
    r = lax.broadcasted_iota(jnp.int32, (rows, width), 0)
    c = lax.broadcasted_iota(jnp.int32, (rows, width), 1)
    return (c // HEAD_DIM) == r


def _fox_dec_kernel(pt_ref, q_ref, kn_ref, vn_ref, lfn_ref, *refs, npg):
    k_refs = refs[:npg]
    v_refs = refs[npg:2 * npg]
    lf_refs = refs[2 * npg:3 * npg]
    o_ref = refs[3 * npg]
    m_ref, l_ref, acc_ref, car_ref = refs[3 * npg + 1:]
    c = pl.program_id(1)
    nc = pl.num_programs(1)

    hm = _head_mask()
    qbd = jnp.where(hm, q_ref[0] * QK_SCALE, 0.0)

    @pl.when(c == 0)
    def _():
        m_ref[...] = jnp.sum(qbd * kn_ref[0], axis=1, keepdims=True)
        l_ref[...] = jnp.ones_like(l_ref)
        acc_ref[...] = jnp.broadcast_to(vn_ref[0], acc_ref.shape)
        car_ref[...] = jnp.zeros_like(car_ref)

    qb16 = qbd.astype(BF16)
    r_i = lax.broadcasted_iota(jnp.int32, (PAGE, PAGE), 0)
    c_i = lax.broadcasted_iota(jnp.int32, (PAGE, PAGE), 1)
    later = jnp.where(r_i > c_i, 1.0, 0.0).astype(BF16)
    lfn = lfn_ref[0]

    carry = car_ref[...]
    scores = []
    for i in range(npg):
        lft = lf_refs[i][0]
        h3, m3, l3 = _split3(lft)
        suffix = _dot(h3, later) + _dot(m3, later) + _dot(l3, later) + carry
        carry = carry + jnp.sum(lft, axis=1, keepdims=True)
        kt = k_refs[i][0].reshape(BRANCH_W, PAGE).astype(BF16)
        s = _dot(qb16, kt)
        scores.append(s + suffix + lfn)
    car_ref[...] = carry

    s_all = jnp.concatenate(scores, axis=1)
    m_prev = m_ref[...]
    m_new = jnp.maximum(m_prev, jnp.max(s_all, axis=1, keepdims=True))
    alpha = jnp.exp(m_prev - m_new)
    p_all = jnp.exp(s_all - m_new)
    l_ref[...] = alpha * l_ref[...] + jnp.sum(p_all, axis=1, keepdims=True)
    acc = alpha * acc_ref[...]
    for i in range(npg):
        p = p_all[:, i * PAGE:(i + 1) * PAGE].astype(BF16)
        vt = v_refs[i][0].reshape(BRANCH_W, PAGE).astype(BF16)
        acc = acc + _dot_nt(p, vt)
    acc_ref[...] = acc
    m_ref[...] = m_new

    @pl.when(c == nc - 1)
    def _():
        o = jnp.where(hm, acc / l_ref[...], 0.0)
        o_ref[0] = jnp.sum(o, axis=0, keepdims=True)


def _fox_dec(page_table, q, k_new, v_new, lf_new, pool_k, pool_v, pool_lft, *, npg):
    nseq, n_pages = page_table.shape
    nchunk = n_pages // npg
    last = n_pages - 1
    pt_flat = page_table.reshape(-1)

    def page_map(i, nd):
        return lambda b, c, pt: (pt[b * n_pages + (last - (c * npg + i))],) + (0,) * (nd - 1)

    seq3 = lambda w: pl.BlockSpec((1, 1, w), lambda b, c, pt: (b, 0, 0))
    page_blk = (1, N_HEADS, HEAD_DIM, PAGE)
    in_specs = [seq3(BRANCH_W), seq3(BRANCH_W), seq3(BRANCH_W),
                pl.BlockSpec((1, N_HEADS, 1), lambda b, c, pt: (b, 0, 0))]
    in_specs += [pl.BlockSpec(page_blk, page_map(i, 4)) for i in range(npg)]
    in_specs += [pl.BlockSpec(page_blk, page_map(i, 4)) for i in range(npg)]
    in_specs += [pl.BlockSpec((1, N_HEADS, PAGE), page_map(i, 3)) for i in range(npg)]
    grid_spec = pltpu.PrefetchScalarGridSpec(
        num_scalar_prefetch=1,
        grid=(nseq, nchunk),
        in_specs=in_specs,
        out_specs=seq3(BRANCH_W),
        scratch_shapes=[pltpu.VMEM((N_HEADS, 1), F32), pltpu.VMEM((N_HEADS, 1), F32),
                        pltpu.VMEM((N_HEADS, BRANCH_W), F32), pltpu.VMEM((N_HEADS, 1), F32)],
    )
    out = pl.pallas_call(
        functools.partial(_fox_dec_kernel, npg=npg),
        grid_spec=grid_spec,
        out_shape=jax.ShapeDtypeStruct((nseq, 1, BRANCH_W), F32),
        compiler_params=_cparams(("arbitrary", "arbitrary")),
        name="fox_dec",
    )(pt_flat, q.reshape(nseq, 1, BRANCH_W), k_new.reshape(nseq, 1, BRANCH_W),
      v_new.reshape(nseq, 1, BRANCH_W), lf_new.reshape(nseq, N_HEADS, 1),
      *([pool_k] * npg), *([pool_v] * npg), *([pool_lft] * npg))
    return out.reshape(nseq, BRANCH_W)


def _moba_sel_kernel(pt_ref, q_ref, *refs, npg, nblk):
    k_refs = refs[:npg]
    idx_ref = refs[npg]
    km_ref = refs[npg + 1]
    c = pl.program_id(1)
    nc = pl.num_programs(1)
    ppb = MOBA_BLOCK // PAGE

    @pl.when(c == 0)
    def _():
        km_ref[...] = jnp.zeros_like(km_ref)

    lane = lax.broadcasted_iota(jnp.int32, (BRANCH_W, LANES), 1)
    km = km_ref[...]
    for n in range(npg // ppb):
        tot = k_refs[n * ppb][0].reshape(BRANCH_W, PAGE)
        for j in range(1, ppb):
            tot = tot + k_refs[n * ppb + j][0].reshape(BRANCH_W, PAGE)
        mean = jnp.sum(tot, axis=1, keepdims=True) * (1.0 / MOBA_BLOCK)
        km = jnp.where(lane == c * (npg // ppb) + n, mean, km)
    km_ref[...] = km

    @pl.when(c == nc - 1)
    def _():
        qbd = jnp.where(_head_mask(), q_ref[0], 0.0)
        s = _dot3(qbd, km)
        col = lax.broadcasted_iota(jnp.int32, s.shape, 1).astype(F32)
        s = jnp.where(col < nblk, s, NEG_INF)
        out = jnp.zeros(s.shape, F32)
        for k in range(MOBA_TOPK):
            m = jnp.max(s, axis=1, keepdims=True)
            first = jnp.min(jnp.where(s == m, col, float(LANES)), axis=1, keepdims=True)
            out = jnp.where(col == k, first, out)
            s = jnp.where(col == first, -3e38, s)
        idx_ref[0] = out.astype(jnp.int32)


def _moba_sel(page_table, q, pool_k, *, npg):
    nseq, n_pages = page_table.shape
    nblk = n_pages * PAGE // MOBA_BLOCK
    assert nblk >= MOBA_TOPK and nblk <= LANES
    pt_flat = page_table.reshape(-1)

    def page_map(i):
        return lambda b, c, pt: (pt[b * n_pages + c * npg + i], 0, 0, 0)

    grid_spec = pltpu.PrefetchScalarGridSpec(
        num_scalar_prefetch=1,
        grid=(nseq, n_pages // npg),
        in_specs=[pl.BlockSpec((1, 1, BRANCH_W), lambda b, c, pt: (b, 0, 0))]
        + [pl.BlockSpec((1, N_HEADS, HEAD_DIM, PAGE), page_map(i)) for i in range(npg)],
        out_specs=pl.BlockSpec((1, N_HEADS, LANES), lambda b, c, pt: (b, 0, 0)),
        scratch_shapes=[pltpu.VMEM((BRANCH_W, LANES), F32)],
    )
    return pl.pallas_call(
        functools.partial(_moba_sel_kernel, npg=npg, nblk=nblk),
        grid_spec=grid_spec,
        out_shape=jax.ShapeDtypeStruct((nseq, N_HEADS, LANES), jnp.int32),
        compiler_params=_cparams(("arbitrary", "arbitrary")),
        name="moba_sel",
    )(pt_flat, q.reshape(nseq, 1, BRANCH_W), *([pool_k] * npg))


MOBA_DEC_HEADS = 4


def _moba_dec_kernel(pt_ref, idx_ref, q_ref, kn_ref, vn_ref, *refs, nsel):
    n_kv = MOBA_DEC_HEADS * nsel
    k_refs = refs[:n_kv]
    v_refs = refs[n_kv:2 * n_kv]
    o_ref = refs[2 * n_kv]
    g = pl.program_id(1)
    for hh in range(MOBA_DEC_HEADS):
        h = g * MOBA_DEC_HEADS + hh
        q = q_ref[0, pl.ds(h, 1), :] * QK_SCALE
        q8 = jnp.broadcast_to(q, (8, HEAD_DIM)).astype(BF16)
        s_new = jnp.sum(q * kn_ref[0, pl.ds(h, 1), :], axis=1, keepdims=True)
        scores = [_dot(q8, k_refs[hh * nsel + i][0, 0].astype(BF16))
                  for i in range(nsel)]
        s_all = jnp.concatenate(scores, axis=1)
        m = jnp.maximum(jnp.max(s_all, axis=1, keepdims=True), s_new)
        p_all = jnp.exp(s_all - m)
        p_new = jnp.exp(s_new - m)
        l = jnp.sum(p_all, axis=1, keepdims=True) + p_new
        acc = p_new * vn_ref[0, pl.ds(h, 1), :]
        for i in range(nsel):
            p = p_all[:, i * PAGE:(i + 1) * PAGE].astype(BF16)
            acc = acc + _dot_nt(p, v_refs[hh * nsel + i][0, 0].astype(BF16))
        o_ref[0, pl.ds(h, 1), :] = (acc / l)[0:1, :]


def _moba_dec(page_table, idx, q, k_new, v_new, pool_k, pool_v):
    nseq, n_pages = page_table.shape
    ppb = MOBA_BLOCK // PAGE
    nsel = MOBA_TOPK * ppb
    pt_flat = page_table.reshape(-1)
    idx_flat = idx[:, :, :MOBA_TOPK].reshape(-1)

    def page_map(hh, k, j):
        def f(b, g, pt, ix):
            h = g * MOBA_DEC_HEADS + hh
            blk = ix[(b * N_HEADS + h) * MOBA_TOPK + k]
            return (pt[b * n_pages + blk * ppb + j], h, 0, 0)
        return f

    seq_spec = pl.BlockSpec((1, N_HEADS, HEAD_DIM), lambda b, g, pt, ix: (b, 0, 0))
    page_blk = (1, 1, HEAD_DIM, PAGE)
    sel = [(hh, k, j) for hh in range(MOBA_DEC_HEADS) for k in range(MOBA_TOPK) for j in range(ppb)]
    page_specs = [pl.BlockSpec(page_blk, page_map(hh, k, j)) for hh, k, j in sel]
    grid_spec = pltpu.PrefetchScalarGridSpec(
        num_scalar_prefetch=2,
        grid=(nseq, N_HEADS // MOBA_DEC_HEADS),
        in_specs=[seq_spec, seq_spec, seq_spec] + page_specs + page_specs,
        out_specs=seq_spec,
    )
    per_head = lambda a: a.reshape(nseq, N_HEADS, HEAD_DIM)
    out = pl.pallas_call(
        functools.partial(_moba_dec_kernel, nsel=nsel),
        grid_spec=grid_spec,
        out_shape=jax.ShapeDtypeStruct((nseq, N_HEADS, HEAD_DIM), F32),
        compiler_params=_cparams(("arbitrary", "arbitrary")),
        name="moba_dec",
    )(pt_flat, idx_flat, per_head(q), per_head(k_new), per_head(v_new),
      *([pool_k] * len(sel)), *([pool_v] * len(sel)))
    return out.reshape(nseq, BRANCH_W)


def kernel(x_prompt, x_sample, c_prompt, c_sample, cache_a_k, cache_a_v, cache_b_k, cache_b_v,
           cache_b_logf, page_table, w_ada, b_ada, g_norm1, g_norm2, g_final, w_in, b_f,
           w_a_o, w_b_o, w_out, w_gate, w_up, w_down):
    depth = w_ada.shape[0]
    assert depth == 1
    batch, seq, _ = x_prompt.shape
    nseq, dec_seq, _ = x_sample.shape
    assert batch == 1 and dec_seq == 1
    n_pool = cache_a_k.shape[1]
    n_pages = page_table.shape[1]
    past_len = n_pages * PAGE
    assert past_len % MOBA_BLOCK == 0

    w_in0 = w_in[0]
    n_qkv = 6 * BRANCH_W
    wqkv = w_in0[:, :n_qkv].astype(BF16)
    wf = jnp.pad(w_in0[:, n_qkv:n_qkv + N_HEADS], ((0, 0), (0, LANES - N_HEADS))).astype(BF16)
    wg = w_in0[:, n_qkv + N_HEADS:].astype(BF16)
    bf = jnp.pad(b_f[0], (0, LANES - N_HEADS)).reshape(1, LANES)
    wao = w_a_o[0].astype(BF16)
    wbo = w_b_o[0].astype(BF16)
    wout = w_out[0].astype(BF16)
    wgate = w_gate[0].astype(BF16)
    wup = w_up[0].astype(BF16)
    wdown = w_down[0].astype(BF16)
    g1 = g_norm1[0].reshape(1, D_MODEL)
    g2 = g_norm2[0].reshape(1, D_MODEL)
    gf = g_final.reshape(1, D_MODEL)
    inv8 = ROPE_THETA ** (-jnp.arange(ROT_HALF, dtype=F32) / ROT_HALF)
    lane = np.arange(LANES)
    inv = jnp.where(jnp.asarray((lane % HEAD_DIM) < ROT_DIM), inv8[lane % ROT_HALF], 0.0)
    inv = inv.reshape(1, LANES).astype(F32)

    pad = (-(batch + nseq)) % 8
    c_all = jnp.concatenate([c_prompt, c_sample, jnp.zeros((pad, D_MODEL), F32)], axis=0)
    mods = _ada(c_all, w_ada[0], b_ada[0])
    mp = [mods[0:1, j * D_MODEL:(j + 1) * D_MODEL] for j in range(6)]
    ms = [mods[batch:batch + nseq, j * D_MODEL:(j + 1) * D_MODEL] for j in range(6)]

    xp = x_prompt.reshape(seq, D_MODEL)
    qa, ka, va, qb, kb, vb, lf, sga, sgb = _inproj(
        xp, mp[0], mp[1], g1, wqkv, wf, wg, bf, inv, tm=256, pos0=0, pos_stride=1)
    qaug, kaug, vaug = _prep(qa, ka, va, qb, kb, vb, lf)
    o_p = _attn(qaug, kaug, vaug, tq=1024)
    y_p = _post(xp, o_p, sga, sgb, mp[2], mp[3], mp[4], mp[5], g2, gf,
                wao, wbo, wout, wgate, wup, wdown, tm=256)

    xs = x_sample.reshape(nseq, D_MODEL)
    qa_s, ka_s, va_s, qb_s, kb_s, vb_s, lf_s, sga_s, sgb_s = _inproj(
        xs, ms[0], ms[1], g1, wqkv, wf, wg, bf, inv, tm=nseq, pos0=past_len, pos_stride=0)
    pool_t = lambda c: jnp.transpose(c[0], (0, 2, 3, 1))
    pool_ak, pool_av = pool_t(cache_a_k), pool_t(cache_a_v)
    pool_bk, pool_bv = pool_t(cache_b_k), pool_t(cache_b_v)
    pool_lft = jnp.transpose(cache_b_logf[0], (0, 2, 1))
    ob_s = _fox_dec(page_table, qb_s, kb_s, vb_s, lf_s[:, :N_HEADS],
                    pool_bk, pool_bv, pool_lft, npg=16)
    idx = _moba_sel(page_table, qa_s, pool_ak, npg=32)
    oa_s = _moba_dec(page_table, idx, qa_s, ka_s, va_s, pool_ak, pool_av)
    o_s = jnp.concatenate([oa_s, ob_s], axis=1)
    y_s = _post(xs, o_s, sga_s, sgb_s, ms[2], ms[3], ms[4], ms[5], g2, gf,
                wao, wbo, wout, wgate, wup, wdown, tm=nseq)

    return (y_p.reshape(batch, seq, D_MODEL),
            y_s.reshape(nseq, dec_seq, D_MODEL),
            ka.reshape(depth, batch, seq, N_HEADS, HEAD_DIM),
            va.reshape(depth, batch, seq, N_HEADS, HEAD_DIM),
            kb.reshape(depth, batch, seq, N_HEADS, HEAD_DIM),
            vb.reshape(depth, batch, seq, N_HEADS, HEAD_DIM),
            lf[:, :N_HEADS].reshape(depth, batch, seq, N_HEADS),
            ka_s.reshape(depth, nseq, dec_seq, N_HEADS, HEAD_DIM),
            va_s.reshape(depth, nseq, dec_seq, N_HEADS, HEAD_DIM),
            kb_s.reshape(depth, nseq, dec_seq, N_HEADS, HEAD_DIM),
            vb_s.reshape(depth, nseq, dec_seq, N_HEADS, HEAD_DIM),
            lf_s[:, :N_HEADS].reshape(depth, nseq, dec_seq, N_HEADS))
```

```python
import functools

import numpy as np
import jax
import jax.numpy as jnp
from jax import lax
from jax.experimental import pallas as pl
from jax.experimental.pallas import tpu as pltpu

F32 = jnp.float32
BF16 = jnp.bfloat16

D_MODEL = 1024
HEAD_DIM = 64
N_HEADS = 8
BRANCH_W = N_HEADS * HEAD_DIM
D_FF = 2816
MOBA_BLOCK = 256
MOBA_TOPK = 3
ROT_DIM = 16
ROT_HALF = 8
ROPE_THETA = 500000.0
RMS_EPS = 1e-6
NEG_INF = -1e30
PAGE = 128
QK_SCALE = HEAD_DIM ** -0.5
LOG2E = 1.4426950408889634
LANES = 128
AUG_W = 128

VMEM_LIMIT = 56 * 1024 * 1024


def _cparams(sem, vmem=VMEM_LIMIT):
    return pltpu.CompilerParams(dimension_semantics=sem, vmem_limit_bytes=vmem)


def _split2(x):
    hi = x.astype(BF16)
    lo = (x - hi.astype(F32)).astype(BF16)
    return hi, lo


def _split3(x):
    hi = x.astype(BF16)
    r = x - hi.astype(F32)
    mid = r.astype(BF16)
    lo = (r - mid.astype(F32)).astype(BF16)
    return hi, mid, lo


def _dot(a, b):
    return jnp.dot(a, b, preferred_element_type=F32)


def _dot_nt(a, b):
    return lax.dot_general(a, b, (((1,), (1,)), ((), ())), preferred_element_type=F32)


def _dot3(a, b):
    ah, al = _split2(a)
    bh, bl = _split2(b)
    return _dot(ah, bh) + _dot(ah, bl) + _dot(al, bh)


def _dot3_nt(a, b):
    ah, al = _split2(a)
    bh, bl = _split2(b)
    return _dot_nt(ah, bh) + _dot_nt(ah, bl) + _dot_nt(al, bh)


def _rms(x, g):
    return x * lax.rsqrt(jnp.mean(x * x, axis=-1, keepdims=True) + RMS_EPS) * g


def _sigmoid(x):
    return 1.0 / (1.0 + jnp.exp(-x))


def _ada_kernel(c_ref, w_ref, b_ref, o_ref):
    c = c_ref[...]
    a = c * _sigmoid(c)
    o_ref[...] = _dot3(a, w_ref[...]) + b_ref[...]


def _ada(c_all, w_ada, b_ada):
    rows = c_all.shape[0]
    n = w_ada.shape[1]
    tn = 1536
    return pl.pallas_call(
        _ada_kernel,
        grid=(n // tn,),
        in_specs=[pl.BlockSpec((rows, D_MODEL), lambda j: (0, 0)),
                  pl.BlockSpec((D_MODEL, tn), lambda j: (0, j)),
                  pl.BlockSpec((1, tn), lambda j: (0, j))],
        out_specs=pl.BlockSpec((rows, tn), lambda j: (0, j)),
        out_shape=jax.ShapeDtypeStruct((rows, n), F32),
        compiler_params=_cparams(("arbitrary",)),
        name="ada",
    )(c_all, w_ada, b_ada.reshape(1, n))


def _inproj_kernel(x_ref, sh_ref, sc_ref, g_ref, wqkv_ref, wf_ref, wg_ref, bf_ref, inv_ref,
                   qa_ref, ka_ref, va_ref, qb_ref, kb_ref, vb_ref, lf_ref, sga_ref, sgb_ref,
                   *, tm, pos0, pos_stride):
    i = pl.program_id(0)
    x = x_ref[...]
    u = _rms(x, g_ref[...]) * (1.0 + sc_ref[...]) + sh_ref[...]
    ub = u.astype(BF16)

    row = lax.broadcasted_iota(jnp.int32, (tm, 1), 0) + i * tm
    pos = (pos0 + pos_stride * row).astype(F32)
    ang = pos * inv_ref[...]
    cos = jnp.cos(ang)
    sin = jnp.sin(ang)
    lane = lax.broadcasted_iota(jnp.int32, (tm, LANES), 1)
    first_half = (lane % HEAD_DIM) < ROT_HALF

    def rope(y):
        parts = []
        for g in range(BRANCH_W // LANES):
            yg = y[:, g * LANES:(g + 1) * LANES]
            partner = jnp.where(first_half,
                                -pltpu.roll(yg, LANES - ROT_HALF, axis=1),
                                pltpu.roll(yg, ROT_HALF, axis=1))
            parts.append(yg * cos + partner * sin)
        return jnp.concatenate(parts, axis=1)

    def proj(j):
        return _dot(ub, wqkv_ref[:, j * BRANCH_W:(j + 1) * BRANCH_W])

    qa_ref[...] = rope(proj(0))
    ka_ref[...] = rope(proj(1))
    va_ref[...] = proj(2)
    qb_ref[...] = proj(3)
    kb_ref[...] = proj(4)
    vb_ref[...] = proj(5)

    z = _dot(ub, wf_ref[...]) + bf_ref[...]
    lf_ref[...] = jnp.minimum(z, 0.0) - jnp.log(1.0 + jnp.exp(-jnp.abs(z)))

    sga_ref[...] = _sigmoid(_dot(ub, wg_ref[:, :D_MODEL]))
    sgb_ref[...] = _sigmoid(_dot(ub, wg_ref[:, D_MODEL:]))


def _inproj(x, sh, sc, g1, wqkv, wf, wg, bf, inv, *, tm, pos0, pos_stride):
    rows = x.shape[0]
    per_row = sh.shape[0] != 1
    mod_spec = (pl.BlockSpec((tm, D_MODEL), lambda i: (i, 0)) if per_row
                else pl.BlockSpec((1, D_MODEL), lambda i: (0, 0)))
    const = lambda shape: pl.BlockSpec(shape, lambda i: (0, 0), pipeline_mode=pl.Buffered(1))
    row_spec = lambda w: pl.BlockSpec((tm, w), lambda i: (i, 0))
    out_w = [BRANCH_W] * 6 + [LANES, D_MODEL, D_MODEL]
    return pl.pallas_call(
        functools.partial(_inproj_kernel, tm=tm, pos0=pos0, pos_stride=pos_stride),
        grid=(rows // tm,),
        in_specs=[row_spec(D_MODEL), mod_spec, mod_spec, const((1, D_MODEL)),
                  const(wqkv.shape), const(wf.shape), const(wg.shape),
                  const((1, LANES)), const((1, LANES))],
        out_specs=[row_spec(w) for w in out_w],
        out_shape=[jax.ShapeDtypeStruct((rows, w), F32) for w in out_w],
        compiler_params=_cparams(("arbitrary",)),
        name="inproj",
    )(x, sh, sc, g1, wqkv, wf, wg, bf, inv)


def _top3_mask(s, n_valid):
    nblk, rows = s.shape
    blk = lax.broadcasted_iota(jnp.int32, (nblk, rows), 0).astype(F32)
    n_valid = n_valid.astype(F32)
    s = jnp.where(blk < n_valid, s, NEG_INF)
    sel = jnp.zeros((nblk, rows), F32)
    for k in range(MOBA_TOPK):
        m = jnp.max(s, axis=0, keepdims=True)
        first = jnp.min(jnp.where(s == m, blk, float(nblk)), axis=0, keepdims=True)
        pick = blk == first
        sel = jnp.where(pick & (n_valid > k), 1.0, sel)
        s = jnp.where(pick, -3e38, s)
    return sel > 0.5


def _prep_kernel(qa_ref, ka_ref, va_ref, qb_ref, kb_ref, vb_ref, lf_ref,
                 qaug_ref, kaug_ref, vaug_ref, km_ref, fc_ref, *, tm, nblk):
    i = pl.program_id(0)

    @pl.when(i == 0)
    def _():
        km_ref[...] = jnp.zeros_like(km_ref)
        fc_ref[...] = jnp.zeros_like(fc_ref)

    lf = lf_ref[...]
    r_i = lax.broadcasted_iota(jnp.int32, (tm, tm), 0)
    c_i = lax.broadcasted_iota(jnp.int32, (tm, tm), 1)
    tri = jnp.where(c_i <= r_i, 1.0, 0.0).astype(BF16)
    l_hi, l_mid, l_lo = _split3(lf)
    fcum = _dot(tri, l_hi) + _dot(tri, l_mid) + _dot(tri, l_lo) + fc_ref[...]
    fc_ref[...] = fcum[tm - 1:tm, :]
    f_hi, f_mid, f_lo = [p.astype(F32) for p in _split3(-fcum * LOG2E)]

    qa = qa_ref[...]
    ka = ka_ref[...]
    va = va_ref[...]
    qb = qb_ref[...]
    kb = kb_ref[...]
    vb = vb_ref[...]
    km = km_ref[...]

    lane64 = lax.broadcasted_iota(jnp.int32, (tm, HEAD_DIM), 1)
    own = lane64 == i
    own_f = jnp.where(own, 1.0, 0.0)
    ones3 = jnp.where(lane64 < 3, 1.0, 0.0)
    ones1 = jnp.where(lane64 < 1, 1.0, 0.0)

    def put(ref, g, left, right):
        ref[:, g * AUG_W:(g + 1) * AUG_W] = jnp.concatenate([left, right], axis=1).astype(BF16)

    own_t = lax.broadcasted_iota(jnp.int32, (HEAD_DIM, tm), 0) == i

    for h in range(N_HEADS):
        hs = slice(h * HEAD_DIM, (h + 1) * HEAD_DIM)
        qh = qa[:, hs]
        s_t = _dot3_nt(km[:, hs], qh)
        sel_t = _top3_mask(s_t, i)
        bias = jnp.where(sel_t | own_t, 0.0, NEG_INF).T
        put(qaug_ref, h, qh * (QK_SCALE * LOG2E), bias)
        put(kaug_ref, h, ka[:, hs], own_f)
        put(vaug_ref, h, va[:, hs], ones1)

        g = N_HEADS + h
        put(qaug_ref, g, qb[:, hs] * (QK_SCALE * LOG2E), ones3)
        dec = jnp.where(lane64 == 0, f_hi[:, h:h + 1],
                        jnp.where(lane64 == 1, f_mid[:, h:h + 1],
                                  jnp.where(lane64 == 2, f_lo[:, h:h + 1], 0.0)))
        put(kaug_ref, g, kb[:, hs], dec)
        put(vaug_ref, g, vb[:, hs], ones1)

    km_ref[pl.ds(i, 1), :] = jnp.mean(ka, axis=0, keepdims=True)


def _prep(qa, ka, va, qb, kb, vb, lf):
    t = qa.shape[0]
    tm = MOBA_BLOCK
    nblk = t // tm
    assert nblk <= HEAD_DIM, "block-selection bias lanes hold at most 64 blocks"
    row = lambda w: pl.BlockSpec((tm, w), lambda i: (i, 0))
    n_aug = 2 * N_HEADS * AUG_W
    return pl.pallas_call(
        functools.partial(_prep_kernel, tm=tm, nblk=nblk),
        grid=(nblk,),
        in_specs=[row(BRANCH_W)] * 6 + [row(LANES)],
        out_specs=[row(n_aug)] * 3,
        out_shape=[jax.ShapeDtypeStruct((t, n_aug), BF16)] * 3,
        scratch_shapes=[pltpu.VMEM((HEAD_DIM, BRANCH_W), F32),
                        pltpu.VMEM((1, LANES), F32)],
        compiler_params=_cparams(("arbitrary",)),
        name="prep",
    )(qa, ka, va, qb, kb, vb, lf)


def _attn_kernel(qi_ref, kj_ref, q_ref, k_ref, v_ref, o_ref, m_ref, acc_ref, *, tq, tk):
    p_id = pl.program_id(1)
    qi = qi_ref[p_id]
    kj = kj_ref[p_id]

    @pl.when(kj == 0)
    def _():
        m_ref[...] = jnp.full_like(m_ref, -jnp.inf)
        acc_ref[...] = jnp.zeros_like(acc_ref)

    def step(diag):
        for hh in range(ATTN_HEADS):
            hs = slice(hh * AUG_W, (hh + 1) * AUG_W)
            s = _dot_nt(q_ref[:, hs], k_ref[:, hs])
            if diag:
                r = lax.broadcasted_iota(jnp.int32, (tq, tk), 0)
                c = lax.broadcasted_iota(jnp.int32, (tq, tk), 1)
                s = jnp.where(c <= r, s, NEG_INF)
            m_prev = m_ref[hh]
            m_new = jnp.maximum(m_prev, jnp.max(s, axis=1, keepdims=True))
            alpha = jnp.exp2(m_prev - m_new)
            p = jnp.concatenate(
                [jnp.exp2(s[:, j * LANES:(j + 1) * LANES] - m_new).astype(BF16)
                 for j in range(tk // LANES)], axis=1)
            acc_ref[hh] = alpha * acc_ref[hh] + _dot(p, v_ref[:, hs])
            m_ref[hh] = m_new

    @pl.when(kj < qi)
    def _():
        step(False)

    @pl.when(kj == qi)
    def _():
        step(True)
        outs = []
        for hh in range(ATTN_HEADS):
            acc = acc_ref[hh]
            outs.append(acc[:, :HEAD_DIM] / acc[:, HEAD_DIM:HEAD_DIM + 1])
        o_ref[...] = jnp.concatenate(outs, axis=1).astype(o_ref.dtype)


ATTN_HEADS = 4


def _attn(qaug, kaug, v, *, tq):
    t = qaug.shape[0]
    tk = tq
    nq = t // tq
    pairs = [(i, j) for i in range(nq) for j in range(i + 1)]
    qi = jnp.asarray(np.array([p[0] for p in pairs], np.int32))
    kj = jnp.asarray(np.array([p[1] for p in pairs], np.int32))
    n_heads = qaug.shape[1] // AUG_W
    n_groups = n_heads // ATTN_HEADS
    w_in = ATTN_HEADS * AUG_W
    w_out = ATTN_HEADS * HEAD_DIM
    grid_spec = pltpu.PrefetchScalarGridSpec(
        num_scalar_prefetch=2,
        grid=(n_groups, len(pairs)),
        in_specs=[pl.BlockSpec((tq, w_in), lambda h, p, qi, kj: (qi[p], h)),
                  pl.BlockSpec((tk, w_in), lambda h, p, qi, kj: (kj[p], h)),
                  pl.BlockSpec((tk, w_in), lambda h, p, qi, kj: (kj[p], h))],
        out_specs=pl.BlockSpec((tq, w_out), lambda h, p, qi, kj: (qi[p], h)),
        scratch_shapes=[pltpu.VMEM((ATTN_HEADS, tq, LANES), F32),
                        pltpu.VMEM((ATTN_HEADS, tq, LANES), F32)],
    )
    return pl.pallas_call(
        functools.partial(_attn_kernel, tq=tq, tk=tk),
        grid_spec=grid_spec,
        out_shape=jax.ShapeDtypeStruct((t, n_heads * HEAD_DIM), BF16),
        compiler_params=_cparams(("arbitrary", "arbitrary")),
        name="attn",
    )(qi, kj, qaug, kaug, v)


def _post_kernel(x_ref, o_ref, sga_ref, sgb_ref, gt1_ref, sh2_ref, sc2_ref, gt2_ref,
                 g2_ref, gf_ref, wao_ref, wbo_ref, wout_ref, wgate_ref, wup_ref, wdown_ref,
                 y_ref):
    o = o_ref[...].astype(BF16)
    ma = _dot(o[:, :BRANCH_W], wao_ref[...])
    mb = _dot(o[:, BRANCH_W:], wbo_ref[...])
    mix = (sga_ref[...] * ma + sgb_ref[...] * mb).astype(BF16)
    h1 = x_ref[...] + gt1_ref[...] * _dot(mix, wout_ref[...])
    u2 = (_rms(h1, g2_ref[...]) * (1.0 + sc2_ref[...]) + sh2_ref[...]).astype(BF16)
    gate = _dot(u2, wgate_ref[...])
    up = _dot(u2, wup_ref[...])
    act = (gate * _sigmoid(gate) * up).astype(BF16)
    h2 = h1 + gt2_ref[...] * _dot(act, wdown_ref[...])
    y_ref[...] = _rms(h2, gf_ref[...])


def _post(x, o, sga, sgb, gt1, sh2, sc2, gt2, g2, gf, wao, wbo, wout, wgate, wup, wdown, *, tm):
    rows = x.shape[0]
    per_row = gt1.shape[0] != 1
    mod_spec = (pl.BlockSpec((tm, D_MODEL), lambda i: (i, 0)) if per_row
                else pl.BlockSpec((1, D_MODEL), lambda i: (0, 0)))
    const = lambda a: pl.BlockSpec(a.shape, lambda i: (0, 0), pipeline_mode=pl.Buffered(1))
    row = pl.BlockSpec((tm, D_MODEL), lambda i: (i, 0))
    return pl.pallas_call(
        _post_kernel,
        grid=(rows // tm,),
        in_specs=[row, row, row, row, mod_spec, mod_spec, mod_spec, mod_spec,
                  const(g2), const(gf), const(wao), const(wbo), const(wout),
                  const(wgate), const(wup), const(wdown)],
        out_specs=row,
        out_shape=jax.ShapeDtypeStruct((rows, D_MODEL), F32),
        compiler_params=_cparams(("arbitrary",)),
        name="post",
    )(x, o, sga, sgb, gt1, sh2, sc2, gt2, g2, gf, wao, wbo, wout, wgate, wup, wdown)


def _head_mask(rows=N_HEADS, width=BRANCH_W):
    r = lax.broadcasted_iota(jnp.int32, (rows, width), 0)
    c = lax.broadcasted_iota(jnp.int32, (rows, width), 1)
    return (c // HEAD_DIM) == r


def _fox_dec_kernel(pt_ref, q_ref, kn_ref, vn_ref, lfn_ref, *refs, npg):
    k_refs = refs[:npg]
    v_refs = refs[npg:2 * npg]
    lf_refs = refs[2 * npg:3 * npg]
    o_ref = refs[3 * npg]
    m_ref, l_ref, acc_ref, car_ref = refs[3 * npg + 1:]
    c = pl.program_id(1)
    nc = pl.num_programs(1)

    hm = _head_mask()
    qbd = jnp.where(hm, q_ref[0] * QK_SCALE, 0.0)

    @pl.when(c == 0)
    def _():
        m_ref[...] = jnp.sum(qbd * kn_ref[0], axis=1, keepdims=True)
        l_ref[...] = jnp.ones_like(l_ref)
        acc_ref[...] = jnp.broadcast_to(vn_ref[0], acc_ref.shape)
        car_ref[...] = jnp.zeros_like(car_ref)

    qb16 = qbd.astype(BF16)
    r_i = lax.broadcasted_iota(jnp.int32, (PAGE, PAGE), 0)
    c_i = lax.broadcasted_iota(jnp.int32, (PAGE, PAGE), 1)
    later = jnp.where(r_i > c_i, 1.0, 0.0).astype(BF16)
    lfn = lfn_ref[0]

    carry = car_ref[...]
    scores = []
    for i in range(npg):
        lft = lf_refs[i][0]
        h3, m3, l3 = _split3(lft)
        suffix = _dot(h3, later) + _dot(m3, later) + _dot(l3, later) + carry
        carry = carry + jnp.sum(lft, axis=1, keepdims=True)
        kt = k_refs[i][0].reshape(BRANCH_W, PAGE).astype(BF16)
        s = _dot(qb16, kt)
        scores.append(s + suffix + lfn)
    car_ref[...] = carry

    s_all = jnp.concatenate(scores, axis=1)
    m_prev = m_ref[...]
    m_new = jnp.maximum(m_prev, jnp.max(s_all, axis=1, keepdims=True))
    alpha = jnp.exp(m_prev - m_new)
    p_all = jnp.exp(s_all - m_new)
    l_ref[...] = alpha * l_ref[...] + jnp.sum(p_all, axis=1, keepdims=True)
    acc = alpha * acc_ref[...]
    for i in range(npg):
        p = p_all[:, i * PAGE:(i + 1) * PAGE].astype(BF16)
        vt = v_refs[i][0].reshape(BRANCH_W, PAGE).astype(BF16)
        acc = acc + _dot_nt(p, vt)
    acc_ref[...] = acc
    m_ref[...] = m_new

    @pl.when(c == nc - 1)
    def _():
        o = jnp.where(hm, acc / l_ref[...], 0.0)
        o_ref[0] = jnp.sum(o, axis=0, keepdims=True)


def _fox_dec(page_table, q, k_new, v_new, lf_new, pool_k, pool_v, pool_lft, *, npg):
    nseq, n_pages = page_table.shape
    nchunk = n_pages // npg
    last = n_pages - 1
    pt_flat = page_table.reshape(-1)

    def page_map(i, nd):
        return lambda b, c, pt: (pt[b * n_pages + (last - (c * npg + i))],) + (0,) * (nd - 1)

    seq3 = lambda w: pl.BlockSpec((1, 1, w), lambda b, c, pt: (b, 0, 0))
    page_blk = (1, N_HEADS, HEAD_DIM, PAGE)
    in_specs = [seq3(BRANCH_W), seq3(BRANCH_W), seq3(BRANCH_W),
                pl.BlockSpec((1, N_HEADS, 1), lambda b, c, pt: (b, 0, 0))]
    in_specs += [pl.BlockSpec(page_blk, page_map(i, 4)) for i in range(npg)]
    in_specs += [pl.BlockSpec(page_blk, page_map(i, 4)) for i in range(npg)]
    in_specs += [pl.BlockSpec((1, N_HEADS, PAGE), page_map(i, 3)) for i in range(npg)]
    grid_spec = pltpu.PrefetchScalarGridSpec(
        num_scalar_prefetch=1,
        grid=(nseq, nchunk),
        in_specs=in_specs,
        out_specs=seq3(BRANCH_W),
        scratch_shapes=[pltpu.VMEM((N_HEADS, 1), F32), pltpu.VMEM((N_HEADS, 1), F32),
                        pltpu.VMEM((N_HEADS, BRANCH_W), F32), pltpu.VMEM((N_HEADS, 1), F32)],
    )
    out = pl.pallas_call(
        functools.partial(_fox_dec_kernel, npg=npg),
        grid_spec=grid_spec,
        out_shape=jax.ShapeDtypeStruct((nseq, 1, BRANCH_W), F32),
        compiler_params=_cparams(("arbitrary", "arbitrary")),
        name="fox_dec",
    )(pt_flat, q.reshape(nseq, 1, BRANCH_W), k_new.reshape(nseq, 1, BRANCH_W),
      v_new.reshape(nseq, 1, BRANCH_W), lf_new.reshape(nseq, N_HEADS, 1),
      *([pool_k] * npg), *([pool_v] * npg), *([pool_lft] * npg))
    return out.reshape(nseq, BRANCH_W)


def _moba_sel_kernel(pt_ref, q_ref, *refs, npg, nblk):
    k_refs = refs[:npg]
    idx_ref = refs[npg]
    km_ref = refs[npg + 1]
    c = pl.program_id(1)
    nc = pl.num_programs(1)
    ppb = MOBA_BLOCK // PAGE

    @pl.when(c == 0)
    def _():
        km_ref[...] = jnp.zeros_like(km_ref)

    lane = lax.broadcasted_iota(jnp.int32, (BRANCH_W, LANES), 1)
    km = km_ref[...]
    for n in range(npg // ppb):
        tot = k_refs[n * ppb][0].reshape(BRANCH_W, PAGE)
        for j in range(1, ppb):
            tot = tot + k_refs[n * ppb + j][0].reshape(BRANCH_W, PAGE)
        mean = jnp.sum(tot, axis=1, keepdims=True) * (1.0 / MOBA_BLOCK)
        km = jnp.where(lane == c * (npg // ppb) + n, mean, km)
    km_ref[...] = km

    @pl.when(c == nc - 1)
    def _():
        qbd = jnp.where(_head_mask(), q_ref[0], 0.0)
        s = _dot3(qbd, km)
        col = lax.broadcasted_iota(jnp.int32, s.shape, 1).astype(F32)
        s = jnp.where(col < nblk, s, NEG_INF)
        out = jnp.zeros(s.shape, F32)
        for k in range(MOBA_TOPK):
            m = jnp.max(s, axis=1, keepdims=True)
            first = jnp.min(jnp.where(s == m, col, float(LANES)), axis=1, keepdims=True)
            out = jnp.where(col == k, first, out)
            s = jnp.where(col == first, -3e38, s)
        idx_ref[0] = out.astype(jnp.int32)


def _moba_sel(page_table, q, pool_k, *, npg):
    nseq, n_pages = page_table.shape
    nblk = n_pages * PAGE // MOBA_BLOCK
    assert nblk >= MOBA_TOPK and nblk <= LANES
    pt_flat = page_table.reshape(-1)

    def page_map(i):
        return lambda b, c, pt: (pt[b * n_pages + c * npg + i], 0, 0, 0)

    grid_spec = pltpu.PrefetchScalarGridSpec(
        num_scalar_prefetch=1,
        grid=(nseq, n_pages // npg),
        in_specs=[pl.BlockSpec((1, 1, BRANCH_W), lambda b, c, pt: (b, 0, 0))]
        + [pl.BlockSpec((1, N_HEADS, HEAD_DIM, PAGE), page_map(i)) for i in range(npg)],
        out_specs=pl.BlockSpec((1, N_HEADS, LANES), lambda b, c, pt: (b, 0, 0)),
        scratch_shapes=[pltpu.VMEM((BRANCH_W, LANES), F32)],
    )
    return pl.pallas_call(
        functools.partial(_moba_sel_kernel, npg=npg, nblk=nblk),
        grid_spec=grid_spec,
        out_shape=jax.ShapeDtypeStruct((nseq, N_HEADS, LANES), jnp.int32),
        compiler_params=_cparams(("arbitrary", "arbitrary")),
        name="moba_sel",
    )(pt_flat, q.reshape(nseq, 1, BRANCH_W), *([pool_k] * npg))


MOBA_DEC_HEADS = 4


def _moba_dec_kernel(phys_ref, q_ref, kn_ref, vn_ref, *refs, nsel):
    n_kv = MOBA_DEC_HEADS * nsel
    k_refs = refs[:n_kv]
    v_refs = refs[n_kv:2 * n_kv]
    o_ref = refs[2 * n_kv]
    g = pl.program_id(1)
    for hh in range(MOBA_DEC_HEADS):
        h = g * MOBA_DEC_HEADS + hh
        q = q_ref[0, pl.ds(h, 1), :] * QK_SCALE
        q8 = jnp.broadcast_to(q, (8, HEAD_DIM)).astype(BF16)
        s_new = jnp.sum(q * kn_ref[0, pl.ds(h, 1), :], axis=1, keepdims=True)
        scores = [_dot(q8, k_refs[hh * nsel + i][0, 0].astype(BF16))
                  for i in range(nsel)]
        s_all = jnp.concatenate(scores, axis=1)
        m = jnp.maximum(jnp.max(s_all, axis=1, keepdims=True), s_new)
        p_all = jnp.exp(s_all - m)
        p_new = jnp.exp(s_new - m)
        l = jnp.sum(p_all, axis=1, keepdims=True) + p_new
        acc = p_new * vn_ref[0, pl.ds(h, 1), :]
        for i in range(nsel):
            p = p_all[:, i * PAGE:(i + 1) * PAGE].astype(BF16)
            acc = acc + _dot_nt(p, v_refs[hh * nsel + i][0, 0].astype(BF16))
        o_ref[0, pl.ds(h, 1), :] = (acc / l)[0:1, :]


def _moba_dec(page_table, idx, q, k_new, v_new, pool_k, pool_v):
    nseq, n_pages = page_table.shape
    ppb = MOBA_BLOCK // PAGE
    nsel = MOBA_TOPK * ppb
    logical = idx[:, :, :MOBA_TOPK, None] * ppb + jnp.arange(ppb, dtype=jnp.int32)
    phys = page_table[jnp.arange(nseq)[:, None, None, None], logical].reshape(-1)

    def page_map(hh, k, j):
        def f(b, g, ph):
            h = g * MOBA_DEC_HEADS + hh
            return (ph[(b * N_HEADS + h) * nsel + k * ppb + j], h, 0, 0)
        return f

    seq_spec = pl.BlockSpec((1, N_HEADS, HEAD_DIM), lambda b, g, ph: (b, 0, 0))
    page_blk = (1, 1, HEAD_DIM, PAGE)
    sel = [(hh, k, j) for hh in range(MOBA_DEC_HEADS) for k in range(MOBA_TOPK) for j in range(ppb)]
    page_specs = [pl.BlockSpec(page_blk, page_map(hh, k, j)) for hh, k, j in sel]
    grid_spec = pltpu.PrefetchScalarGridSpec(
        num_scalar_prefetch=1,
        grid=(nseq, N_HEADS // MOBA_DEC_HEADS),
        in_specs=[seq_spec, seq_spec, seq_spec] + page_specs + page_specs,
        out_specs=seq_spec,
    )
    per_head = lambda a: a.reshape(nseq, N_HEADS, HEAD_DIM)
    out = pl.pallas_call(
        functools.partial(_moba_dec_kernel, nsel=nsel),
        grid_spec=grid_spec,
        out_shape=jax.ShapeDtypeStruct((nseq, N_HEADS, HEAD_DIM), F32),
        compiler_params=_cparams(("arbitrary", "arbitrary")),
        name="moba_dec",
    )(phys, per_head(q), per_head(k_new), per_head(v_new),
      *([pool_k] * len(sel)), *([pool_v] * len(sel)))
    return out.reshape(nseq, BRANCH_W)


def kernel(x_prompt, x_sample, c_prompt, c_sample, cache_a_k, cache_a_v, cache_b_k, cache_b_v,
           cache_b_logf, page_table, w_ada, b_ada, g_norm1, g_norm2, g_final, w_in, b_f,
           w_a_o, w_b_o, w_out, w_gate, w_up, w_down):
    depth = w_ada.shape[0]
    assert depth == 1
    batch, seq, _ = x_prompt.shape
    nseq, dec_seq, _ = x_sample.shape
    assert batch == 1 and dec_seq == 1
    n_pool = cache_a_k.shape[1]
    n_pages = page_table.shape[1]
    past_len = n_pages * PAGE
    assert past_len % MOBA_BLOCK == 0

    w_in0 = w_in[0]
    n_qkv = 6 * BRANCH_W
    wqkv = w_in0[:, :n_qkv].astype(BF16)
    wf = jnp.pad(w_in0[:, n_qkv:n_qkv + N_HEADS], ((0, 0), (0, LANES - N_HEADS))).astype(BF16)
    wg = w_in0[:, n_qkv + N_HEADS:].astype(BF16)
    bf = jnp.pad(b_f[0], (0, LANES - N_HEADS)).reshape(1, LANES)
    wao = w_a_o[0].astype(BF16)
    wbo = w_b_o[0].astype(BF16)
    wout = w_out[0].astype(BF16)
    wgate = w_gate[0].astype(BF16)
    wup = w_up[0].astype(BF16)
    wdown = w_down[0].astype(BF16)
    g1 = g_norm1[0].reshape(1, D_MODEL)
    g2 = g_norm2[0].reshape(1, D_MODEL)
    gf = g_final.reshape(1, D_MODEL)
    inv8 = ROPE_THETA ** (-jnp.arange(ROT_HALF, dtype=F32) / ROT_HALF)
    lane = np.arange(LANES)
    inv = jnp.where(jnp.asarray((lane % HEAD_DIM) < ROT_DIM), inv8[lane % ROT_HALF], 0.0)
    inv = inv.reshape(1, LANES).astype(F32)

    pad = (-(batch + nseq)) % 8
    c_all = jnp.concatenate([c_prompt, c_sample, jnp.zeros((pad, D_MODEL), F32)], axis=0)
    mods = _ada(c_all, w_ada[0], b_ada[0])
    mp = [mods[0:1, j * D_MODEL:(j + 1) * D_MODEL] for j in range(6)]
    ms = [mods[batch:batch + nseq, j * D_MODEL:(j + 1) * D_MODEL] for j in range(6)]

    xp = x_prompt.reshape(seq, D_MODEL)
    qa, ka, va, qb, kb, vb, lf, sga, sgb = _inproj(
        xp, mp[0], mp[1], g1, wqkv, wf, wg, bf, inv, tm=256, pos0=0, pos_stride=1)
    qaug, kaug, vaug = _prep(qa, ka, va, qb, kb, vb, lf)
    o_p = _attn(qaug, kaug, vaug, tq=1024)
    y_p = _post(xp, o_p, sga, sgb, mp[2], mp[3], mp[4], mp[5], g2, gf,
                wao, wbo, wout, wgate, wup, wdown, tm=256)

    xs = x_sample.reshape(nseq, D_MODEL)
    qa_s, ka_s, va_s, qb_s, kb_s, vb_s, lf_s, sga_s, sgb_s = _inproj(
        xs, ms[0], ms[1], g1, wqkv, wf, wg, bf, inv, tm=nseq, pos0=past_len, pos_stride=0)
    pool_t = lambda c: jnp.transpose(c[0], (0, 2, 3, 1))
    pool_ak, pool_av = pool_t(cache_a_k), pool_t(cache_a_v)
    pool_bk, pool_bv = pool_t(cache_b_k), pool_t(cache_b_v)
    pool_lft = jnp.transpose(cache_b_logf[0], (0, 2, 1))
    ob_s = _fox_dec(page_table, qb_s, kb_s, vb_s, lf_s[:, :N_HEADS],
                    pool_bk, pool_bv, pool_lft, npg=32)
    idx = _moba_sel(page_table, qa_s, pool_ak, npg=64)
    oa_s = _moba_dec(page_table, idx, qa_s, ka_s, va_s, pool_ak, pool_av)
    o_s = jnp.concatenate([oa_s, ob_s], axis=1)
    y_s = _post(xs, o_s, sga_s, sgb_s, ms[2], ms[3], ms[4], ms[5], g2, gf,
                wao, wbo, wout, wgate, wup, wdown, tm=nseq)

    return (y_p.reshape(batch, seq, D_MODEL),
            y_s.reshape(nseq, dec_seq, D_MODEL),
            ka.reshape(depth, batch, seq, N_HEADS, HEAD_DIM),
            va.reshape(depth, batch, seq, N_HEADS, HEAD_DIM),
            kb.reshape(depth, batch, seq, N_HEADS, HEAD_DIM),
            vb.reshape(depth, batch, seq, N_HEADS, HEAD_DIM),
            lf[:, :N_HEADS].reshape(depth, batch, seq, N_HEADS),
            ka_s.reshape(depth, nseq, dec_seq, N_HEADS, HEAD_DIM),
            va_s.reshape(depth, nseq, dec_seq, N_HEADS, HEAD_DIM),
            kb_s.reshape(depth, nseq, dec_seq, N_HEADS, HEAD_DIM),
            vb_s.reshape(depth, nseq, dec_seq, N_HEADS, HEAD_DIM),
            lf_s[:, :N_HEADS].reshape(depth, nseq, dec_seq, N_HEADS))
```

```python
import functools

import numpy as np
import jax
import jax.numpy as jnp
from jax import lax
from jax.experimental import pallas as pl
from jax.experimental.pallas import tpu as pltpu

F32 = jnp.float32
BF16 = jnp.bfloat16

D_MODEL = 1024
HEAD_DIM = 64
N_HEADS = 8
BRANCH_W = N_HEADS * HEAD_DIM
D_FF = 2816
MOBA_BLOCK = 256
MOBA_TOPK = 3
ROT_DIM = 16
ROT_HALF = 8
ROPE_THETA = 500000.0
RMS_EPS = 1e-6
NEG_INF = -1e30
PAGE = 128
QK_SCALE = HEAD_DIM ** -0.5
LOG2E = 1.4426950408889634
LANES = 128
AUG_W = 128

VMEM_LIMIT = 56 * 1024 * 1024


def _cparams(sem, vmem=VMEM_LIMIT):
    return pltpu.CompilerParams(dimension_semantics=sem, vmem_limit_bytes=vmem)


def _split2(x):
    hi = x.astype(BF16)
    lo = (x - hi.astype(F32)).astype(BF16)
    return hi, lo


def _split3(x):
    hi = x.astype(BF16)
    r = x - hi.astype(F32)
    mid = r.astype(BF16)
    lo = (r - mid.astype(F32)).astype(BF16)
    return hi, mid, lo


def _dot(a, b):
    return jnp.dot(a, b, preferred_element_type=F32)


def _dot_nt(a, b):
    return lax.dot_general(a, b, (((1,), (1,)), ((), ())), preferred_element_type=F32)


def _dot3(a, b):
    ah, al = _split2(a)
    bh, bl = _split2(b)
    return _dot(ah, bh) + _dot(ah, bl) + _dot(al, bh)


def _dot3_nt(a, b):
    ah, al = _split2(a)
    bh, bl = _split2(b)
    return _dot_nt(ah, bh) + _dot_nt(ah, bl) + _dot_nt(al, bh)


def _rms(x, g):
    return x * lax.rsqrt(jnp.mean(x * x, axis=-1, keepdims=True) + RMS_EPS) * g


def _sigmoid(x):
    return 1.0 / (1.0 + jnp.exp(-x))


def _ada_kernel(c_ref, w_ref, b_ref, o_ref):
    c = c_ref[...]
    a = c * _sigmoid(c)
    o_ref[...] = _dot3(a, w_ref[...]) + b_ref[...]


def _ada(c_all, w_ada, b_ada):
    rows = c_all.shape[0]
    n = w_ada.shape[1]
    tn = 1536
    return pl.pallas_call(
        _ada_kernel,
        grid=(n // tn,),
        in_specs=[pl.BlockSpec((rows, D_MODEL), lambda j: (0, 0)),
                  pl.BlockSpec((D_MODEL, tn), lambda j: (0, j)),
                  pl.BlockSpec((1, tn), lambda j: (0, j))],
        out_specs=pl.BlockSpec((rows, tn), lambda j: (0, j)),
        out_shape=jax.ShapeDtypeStruct((rows, n), F32),
        compiler_params=_cparams(("arbitrary",)),
        name="ada",
    )(c_all, w_ada, b_ada.reshape(1, n))


def _inproj_kernel(x_ref, sh_ref, sc_ref, g_ref, wqkv_ref, wf_ref, wg_ref, bf_ref, inv_ref,
                   qa_ref, ka_ref, va_ref, qb_ref, kb_ref, vb_ref, lf_ref, sga_ref, sgb_ref,
                   *, tm, pos0, pos_stride):
    i = pl.program_id(0)
    x = x_ref[...]
    u = _rms(x, g_ref[...]) * (1.0 + sc_ref[...]) + sh_ref[...]
    ub = u.astype(BF16)

    row = lax.broadcasted_iota(jnp.int32, (tm, 1), 0) + i * tm
    pos = (pos0 + pos_stride * row).astype(F32)
    ang = pos * inv_ref[...]
    cos = jnp.cos(ang)
    sin = jnp.sin(ang)
    lane = lax.broadcasted_iota(jnp.int32, (tm, LANES), 1)
    first_half = (lane % HEAD_DIM) < ROT_HALF

    def rope(y):
        parts = []
        for g in range(BRANCH_W // LANES):
            yg = y[:, g * LANES:(g + 1) * LANES]
            partner = jnp.where(first_half,
                                -pltpu.roll(yg, LANES - ROT_HALF, axis=1),
                                pltpu.roll(yg, ROT_HALF, axis=1))
            parts.append(yg * cos + partner * sin)
        return jnp.concatenate(parts, axis=1)

    def proj(j):
        return _dot(ub, wqkv_ref[:, j * BRANCH_W:(j + 1) * BRANCH_W])

    qa_ref[...] = rope(proj(0))
    ka_ref[...] = rope(proj(1))
    va_ref[...] = proj(2)
    qb_ref[...] = proj(3)
    kb_ref[...] = proj(4)
    vb_ref[...] = proj(5)

    z = _dot(ub, wf_ref[...]) + bf_ref[...]
    lf_ref[...] = jnp.minimum(z, 0.0) - jnp.log(1.0 + jnp.exp(-jnp.abs(z)))

    sga_ref[...] = _sigmoid(_dot(ub, wg_ref[:, :D_MODEL]))
    sgb_ref[...] = _sigmoid(_dot(ub, wg_ref[:, D_MODEL:]))


def _inproj(x, sh, sc, g1, wqkv, wf, wg, bf, inv, *, tm, pos0, pos_stride):
    rows = x.shape[0]
    per_row = sh.shape[0] != 1
    mod_spec = (pl.BlockSpec((tm, D_MODEL), lambda i: (i, 0)) if per_row
                else pl.BlockSpec((1, D_MODEL), lambda i: (0, 0)))
    const = lambda shape: pl.BlockSpec(shape, lambda i: (0, 0), pipeline_mode=pl.Buffered(1))
    row_spec = lambda w: pl.BlockSpec((tm, w), lambda i: (i, 0))
    out_w = [BRANCH_W] * 6 + [LANES, D_MODEL, D_MODEL]
    return pl.pallas_call(
        functools.partial(_inproj_kernel, tm=tm, pos0=pos0, pos_stride=pos_stride),
        grid=(rows // tm,),
        in_specs=[row_spec(D_MODEL), mod_spec, mod_spec, const((1, D_MODEL)),
                  const(wqkv.shape), const(wf.shape), const(wg.shape),
                  const((1, LANES)), const((1, LANES))],
        out_specs=[row_spec(w) for w in out_w],
        out_shape=[jax.ShapeDtypeStruct((rows, w), F32) for w in out_w],
        compiler_params=_cparams(("arbitrary",)),
        name="inproj",
    )(x, sh, sc, g1, wqkv, wf, wg, bf, inv)


def _top3_mask(s, n_valid):
    nblk, rows = s.shape
    blk = lax.broadcasted_iota(jnp.int32, (nblk, rows), 0).astype(F32)
    n_valid = n_valid.astype(F32)
    s = jnp.where(blk < n_valid, s, NEG_INF)
    sel = jnp.zeros((nblk, rows), F32)
    for k in range(MOBA_TOPK):
        m = jnp.max(s, axis=0, keepdims=True)
        first = jnp.min(jnp.where(s == m, blk, float(nblk)), axis=0, keepdims=True)
        pick = blk == first
        sel = jnp.where(pick & (n_valid > k), 1.0, sel)
        s = jnp.where(pick, -3e38, s)
    return sel > 0.5


def _prep_kernel(qa_ref, ka_ref, va_ref, qb_ref, kb_ref, vb_ref, lf_ref,
                 qaug_ref, kaug_ref, vaug_ref, km_ref, fc_ref, *, tm, nblk):
    i = pl.program_id(0)

    @pl.when(i == 0)
    def _():
        km_ref[...] = jnp.zeros_like(km_ref)
        fc_ref[...] = jnp.zeros_like(fc_ref)

    lf = lf_ref[...]
    r_i = lax.broadcasted_iota(jnp.int32, (tm, tm), 0)
    c_i = lax.broadcasted_iota(jnp.int32, (tm, tm), 1)
    tri = jnp.where(c_i <= r_i, 1.0, 0.0).astype(BF16)
    l_hi, l_mid, l_lo = _split3(lf)
    fcum = _dot(tri, l_hi) + _dot(tri, l_mid) + _dot(tri, l_lo) + fc_ref[...]
    fc_ref[...] = fcum[tm - 1:tm, :]
    f_hi, f_mid, f_lo = [p.astype(F32) for p in _split3(-fcum * LOG2E)]

    qa = qa_ref[...]
    ka = ka_ref[...]
    va = va_ref[...]
    qb = qb_ref[...]
    kb = kb_ref[...]
    vb = vb_ref[...]
    km = km_ref[...]

    lane64 = lax.broadcasted_iota(jnp.int32, (tm, HEAD_DIM), 1)
    own = lane64 == i
    own_f = jnp.where(own, 1.0, 0.0)
    ones3 = jnp.where(lane64 < 3, 1.0, 0.0)
    ones1 = jnp.where(lane64 < 1, 1.0, 0.0)

    def put(ref, g, left, right):
        ref[:, g * AUG_W:(g + 1) * AUG_W] = jnp.concatenate([left, right], axis=1).astype(BF16)

    own_t = lax.broadcasted_iota(jnp.int32, (HEAD_DIM, tm), 0) == i

    for h in range(N_HEADS):
        hs = slice(h * HEAD_DIM, (h + 1) * HEAD_DIM)
        qh = qa[:, hs]
        s_t = _dot3_nt(km[:, hs], qh)
        sel_t = _top3_mask(s_t, i)
        bias = jnp.where(sel_t | own_t, 0.0, NEG_INF).T
        put(qaug_ref, h, qh * (QK_SCALE * LOG2E), bias)
        put(kaug_ref, h, ka[:, hs], own_f)
        put(vaug_ref, h, va[:, hs], ones1)

        g = N_HEADS + h
        put(qaug_ref, g, qb[:, hs] * (QK_SCALE * LOG2E), ones3)
        dec = jnp.where(lane64 == 0, f_hi[:, h:h + 1],
                        jnp.where(lane64 == 1, f_mid[:, h:h + 1],
                                  jnp.where(lane64 == 2, f_lo[:, h:h + 1], 0.0)))
        put(kaug_ref, g, kb[:, hs], dec)
        put(vaug_ref, g, vb[:, hs], ones1)

    km_ref[pl.ds(i, 1), :] = jnp.mean(ka, axis=0, keepdims=True)


def _prep(qa, ka, va, qb, kb, vb, lf):
    t = qa.shape[0]
    tm = MOBA_BLOCK
    nblk = t // tm
    assert nblk <= HEAD_DIM, "block-selection bias lanes hold at most 64 blocks"
    row = lambda w: pl.BlockSpec((tm, w), lambda i: (i, 0))
    n_aug = 2 * N_HEADS * AUG_W
    return pl.pallas_call(
        functools.partial(_prep_kernel, tm=tm, nblk=nblk),
        grid=(nblk,),
        in_specs=[row(BRANCH_W)] * 6 + [row(LANES)],
        out_specs=[row(n_aug)] * 3,
        out_shape=[jax.ShapeDtypeStruct((t, n_aug), BF16)] * 3,
        scratch_shapes=[pltpu.VMEM((HEAD_DIM, BRANCH_W), F32),
                        pltpu.VMEM((1, LANES), F32)],
        compiler_params=_cparams(("arbitrary",)),
        name="prep",
    )(qa, ka, va, qb, kb, vb, lf)


ATTN_HEADS = 4


def _attn_body(qi_ref, kj_ref, q_ref, k_ref, v_ref, o_ref, m_ref, acc_ref, *, tq, tk, extra):
    p_id = pl.program_id(1)
    qi = qi_ref[p_id]
    kj = kj_ref[p_id]

    @pl.when(kj == 0)
    def _():
        m_ref[...] = jnp.full_like(m_ref, -jnp.inf)
        acc_ref[...] = jnp.zeros_like(acc_ref)

    def step(diag):
        extra()
        for hh in range(ATTN_HEADS):
            hs = slice(hh * AUG_W, (hh + 1) * AUG_W)
            s = _dot_nt(q_ref[:, hs], k_ref[:, hs])
            if diag:
                r = lax.broadcasted_iota(jnp.int32, (tq, tk), 0)
                c = lax.broadcasted_iota(jnp.int32, (tq, tk), 1)
                s = jnp.where(c <= r, s, NEG_INF)
            m_prev = m_ref[hh]
            m_new = jnp.maximum(m_prev, jnp.max(s, axis=1, keepdims=True))
            alpha = jnp.exp2(m_prev - m_new)
            p = jnp.concatenate(
                [jnp.exp2(s[:, j * LANES:(j + 1) * LANES] - m_new).astype(BF16)
                 for j in range(tk // LANES)], axis=1)
            acc_ref[hh] = alpha * acc_ref[hh] + _dot(p, v_ref[:, hs])
            m_ref[hh] = m_new

    @pl.when(kj < qi)
    def _():
        step(False)

    @pl.when(kj == qi)
    def _():
        step(True)
        outs = []
        for hh in range(ATTN_HEADS):
            acc = acc_ref[hh]
            outs.append(acc[:, :HEAD_DIM] / acc[:, HEAD_DIM:HEAD_DIM + 1])
        o_ref[...] = jnp.concatenate(outs, axis=1).astype(o_ref.dtype)


def _post_kernel(x_ref, o_ref, sga_ref, sgb_ref, gt1_ref, sh2_ref, sc2_ref, gt2_ref,
                 g2_ref, gf_ref, wao_ref, wbo_ref, wout_ref, wgate_ref, wup_ref, wdown_ref,
                 y_ref):
    o = o_ref[...].astype(BF16)
    ma = _dot(o[:, :BRANCH_W], wao_ref[...])
    mb = _dot(o[:, BRANCH_W:], wbo_ref[...])
    mix = (sga_ref[...] * ma + sgb_ref[...] * mb).astype(BF16)
    h1 = x_ref[...] + gt1_ref[...] * _dot(mix, wout_ref[...])
    u2 = (_rms(h1, g2_ref[...]) * (1.0 + sc2_ref[...]) + sh2_ref[...]).astype(BF16)
    gate = _dot(u2, wgate_ref[...])
    up = _dot(u2, wup_ref[...])
    act = (gate * _sigmoid(gate) * up).astype(BF16)
    h2 = h1 + gt2_ref[...] * _dot(act, wdown_ref[...])
    y_ref[...] = _rms(h2, gf_ref[...])


def _post(x, o, sga, sgb, gt1, sh2, sc2, gt2, g2, gf, wao, wbo, wout, wgate, wup, wdown, *, tm):
    rows = x.shape[0]
    per_row = gt1.shape[0] != 1
    mod_spec = (pl.BlockSpec((tm, D_MODEL), lambda i: (i, 0)) if per_row
                else pl.BlockSpec((1, D_MODEL), lambda i: (0, 0)))
    const = lambda a: pl.BlockSpec(a.shape, lambda i: (0, 0), pipeline_mode=pl.Buffered(1))
    row = pl.BlockSpec((tm, D_MODEL), lambda i: (i, 0))
    return pl.pallas_call(
        _post_kernel,
        grid=(rows // tm,),
        in_specs=[row, row, row, row, mod_spec, mod_spec, mod_spec, mod_spec,
                  const(g2), const(gf), const(wao), const(wbo), const(wout),
                  const(wgate), const(wup), const(wdown)],
        out_specs=row,
        out_shape=jax.ShapeDtypeStruct((rows, D_MODEL), F32),
        compiler_params=_cparams(("arbitrary",)),
        name="post",
    )(x, o, sga, sgb, gt1, sh2, sc2, gt2, g2, gf, wao, wbo, wout, wgate, wup, wdown)


def _head_mask(rows=N_HEADS, width=BRANCH_W):
    r = lax.broadcasted_iota(jnp.int32, (rows, width), 0)
    c = lax.broadcasted_iota(jnp.int32, (rows, width), 1)
    return (c // HEAD_DIM) == r


def _fox_qbd(q_ref):
    return jnp.where(_head_mask(), q_ref[0] * QK_SCALE, 0.0)


def _fox_init(c, q_ref, kn_ref, vn_ref, m_ref, l_ref, acc_ref, car_ref):
    @pl.when(c == 0)
    def _():
        m_ref[...] = jnp.sum(_fox_qbd(q_ref) * kn_ref[0], axis=1, keepdims=True)
        l_ref[...] = jnp.ones_like(l_ref)
        acc_ref[...] = jnp.broadcast_to(vn_ref[0], acc_ref.shape)
        car_ref[...] = jnp.zeros_like(car_ref)


def _fox_final(c, nc, o_ref, l_ref, acc_ref):
    @pl.when(c == nc - 1)
    def _():
        o = jnp.where(_head_mask(), acc_ref[...] / l_ref[...], 0.0)
        o_ref[0] = jnp.sum(o, axis=0, keepdims=True)


def _fox_main(q_ref, lfn_ref, k_refs, v_refs, lf_refs, m_ref, l_ref, acc_ref, car_ref):
    npg = len(k_refs)
    qb16 = _fox_qbd(q_ref).astype(BF16)
    r_i = lax.broadcasted_iota(jnp.int32, (PAGE, PAGE), 0)
    c_i = lax.broadcasted_iota(jnp.int32, (PAGE, PAGE), 1)
    later = jnp.where(r_i > c_i, 1.0, 0.0).astype(BF16)
    lfn = lfn_ref[0]

    carry = car_ref[...]
    scores = []
    for i in range(npg):
        lft = lf_refs[i][0]
        h3, m3, l3 = _split3(lft)
        suffix = _dot(h3, later) + _dot(m3, later) + _dot(l3, later) + carry
        carry = carry + jnp.sum(lft, axis=1, keepdims=True)
        kt = k_refs[i][0].reshape(BRANCH_W, PAGE).astype(BF16)
        s = _dot(qb16, kt)
        scores.append(s + suffix + lfn)
    car_ref[...] = carry

    s_all = jnp.concatenate(scores, axis=1)
    m_prev = m_ref[...]
    m_new = jnp.maximum(m_prev, jnp.max(s_all, axis=1, keepdims=True))
    alpha = jnp.exp(m_prev - m_new)
    p_all = jnp.exp(s_all - m_new)
    l_ref[...] = alpha * l_ref[...] + jnp.sum(p_all, axis=1, keepdims=True)
    acc = alpha * acc_ref[...]
    for i in range(npg):
        p = p_all[:, i * PAGE:(i + 1) * PAGE].astype(BF16)
        vt = v_refs[i][0].reshape(BRANCH_W, PAGE).astype(BF16)
        acc = acc + _dot_nt(p, vt)
    acc_ref[...] = acc
    m_ref[...] = m_new


FOX_SCRATCH = [pltpu.VMEM((N_HEADS, 1), F32), pltpu.VMEM((N_HEADS, 1), F32),
               pltpu.VMEM((N_HEADS, BRANCH_W), F32), pltpu.VMEM((N_HEADS, 1), F32)]


def _sel_init(c, km_ref):
    @pl.when(c == 0)
    def _():
        km_ref[...] = jnp.zeros_like(km_ref)


def _sel_main(c, k_refs, km_ref):
    npg = len(k_refs)
    ppb = MOBA_BLOCK // PAGE
    lane = lax.broadcasted_iota(jnp.int32, (BRANCH_W, LANES), 1)
    km = km_ref[...]
    for n in range(npg // ppb):
        tot = k_refs[n * ppb][0].reshape(BRANCH_W, PAGE)
        for j in range(1, ppb):
            tot = tot + k_refs[n * ppb + j][0].reshape(BRANCH_W, PAGE)
        mean = jnp.sum(tot, axis=1, keepdims=True) * (1.0 / MOBA_BLOCK)
        km = jnp.where(lane == c * (npg // ppb) + n, mean, km)
    km_ref[...] = km


def _sel_final(c, nc, nblk, q_ref, idx_ref, km_ref):
    @pl.when(c == nc - 1)
    def _():
        qbd = jnp.where(_head_mask(), q_ref[0], 0.0)
        s = _dot3(qbd, km_ref[...])
        col = lax.broadcasted_iota(jnp.int32, s.shape, 1).astype(F32)
        s = jnp.where(col < nblk, s, NEG_INF)
        out = jnp.zeros(s.shape, F32)
        for k in range(MOBA_TOPK):
            m = jnp.max(s, axis=1, keepdims=True)
            first = jnp.min(jnp.where(s == m, col, float(LANES)), axis=1, keepdims=True)
            out = jnp.where(col == k, first, out)
            s = jnp.where(col == first, -3e38, s)
        idx_ref[0] = out.astype(jnp.int32)


def _attn_dec_kernel(qi_ref, kj_ref, pt_ref, q_ref, k_ref, v_ref,
                     fq_ref, fkn_ref, fvn_ref, flfn_ref, sq_ref, *refs,
                     tq, tk, npg, nc, nblk, n_pairs):
    fk_refs = refs[:npg]
    fv_refs = refs[npg:2 * npg]
    flf_refs = refs[2 * npg:3 * npg]
    sk_refs = refs[3 * npg:4 * npg]
    o_ref, ob_ref, idx_ref = refs[4 * npg:4 * npg + 3]
    m_ref, acc_ref, fm_ref, fl_ref, facc_ref, fcar_ref, km_ref = refs[4 * npg + 3:]
    sidx = pl.program_id(0) * n_pairs + pl.program_id(1)
    c = sidx % nc

    _fox_init(c, fq_ref, fkn_ref, fvn_ref, fm_ref, fl_ref, facc_ref, fcar_ref)
    _sel_init(c, km_ref)

    def decode():
        _fox_main(fq_ref, flfn_ref, fk_refs, fv_refs, flf_refs, fm_ref, fl_ref, facc_ref, fcar_ref)
        _sel_main(c, sk_refs, km_ref)

    _attn_body(qi_ref, kj_ref, q_ref, k_ref, v_ref, o_ref, m_ref, acc_ref, tq=tq, tk=tk,
               extra=decode)
    _fox_final(c, nc, ob_ref, fl_ref, facc_ref)
    _sel_final(c, nc, nblk, sq_ref, idx_ref, km_ref)


def _attn_dec(qaug, kaug, vaug, page_table, qb_s, kb_s, vb_s, lf_s, qa_s,
              pool_bk, pool_bv, pool_lft, pool_ak, *, tq, npg):
    t = qaug.shape[0]
    tk = tq
    nq = t // tq
    pairs = [(i, j) for i in range(nq) for j in range(i + 1)]
    n_pairs = len(pairs)
    qi = jnp.asarray(np.array([p[0] for p in pairs], np.int32))
    kj = jnp.asarray(np.array([p[1] for p in pairs], np.int32))
    n_heads = qaug.shape[1] // AUG_W
    n_groups = n_heads // ATTN_HEADS
    w_in = ATTN_HEADS * AUG_W
    w_out = ATTN_HEADS * HEAD_DIM

    nseq, n_pages = page_table.shape
    nc = n_pages // npg
    nblk = n_pages * PAGE // MOBA_BLOCK
    last = n_pages - 1
    assert nblk >= MOBA_TOPK and nblk <= LANES
    assert n_groups * n_pairs >= nseq * nc, "not enough attention steps to stream the cache"
    pt_flat = page_table.reshape(-1)

    def seq_of(h, p):
        return jnp.minimum((h * n_pairs + p) // nc, nseq - 1)

    def chunk_of(h, p):
        return (h * n_pairs + p) % nc

    def fox_page(i, nd):
        return lambda h, p, qi, kj, pt: (
            pt[seq_of(h, p) * n_pages + (last - (chunk_of(h, p) * npg + i))],) + (0,) * (nd - 1)

    def sel_page(i):
        return lambda h, p, qi, kj, pt: (
            pt[seq_of(h, p) * n_pages + chunk_of(h, p) * npg + i], 0, 0, 0)

    seq3 = lambda w: pl.BlockSpec((1, 1, w), lambda h, p, qi, kj, pt: (seq_of(h, p), 0, 0))
    page_blk = (1, N_HEADS, HEAD_DIM, PAGE)
    in_specs = [pl.BlockSpec((tq, w_in), lambda h, p, qi, kj, pt: (qi[p], h)),
                pl.BlockSpec((tk, w_in), lambda h, p, qi, kj, pt: (kj[p], h)),
                pl.BlockSpec((tk, w_in), lambda h, p, qi, kj, pt: (kj[p], h)),
                seq3(BRANCH_W), seq3(BRANCH_W), seq3(BRANCH_W),
                pl.BlockSpec((1, N_HEADS, 1), lambda h, p, qi, kj, pt: (seq_of(h, p), 0, 0)),
                seq3(BRANCH_W)]
    in_specs += [pl.BlockSpec(page_blk, fox_page(i, 4)) for i in range(npg)]
    in_specs += [pl.BlockSpec(page_blk, fox_page(i, 4)) for i in range(npg)]
    in_specs += [pl.BlockSpec((1, N_HEADS, PAGE), fox_page(i, 3)) for i in range(npg)]
    in_specs += [pl.BlockSpec(page_blk, sel_page(i)) for i in range(npg)]
    out_specs = [pl.BlockSpec((tq, w_out), lambda h, p, qi, kj, pt: (qi[p], h)),
                 seq3(BRANCH_W),
                 pl.BlockSpec((1, N_HEADS, LANES), lambda h, p, qi, kj, pt: (seq_of(h, p), 0, 0))]
    grid_spec = pltpu.PrefetchScalarGridSpec(
        num_scalar_prefetch=3,
        grid=(n_groups, n_pairs),
        in_specs=in_specs,
        out_specs=out_specs,
        scratch_shapes=[pltpu.VMEM((ATTN_HEADS, tq, LANES), F32),
                        pltpu.VMEM((ATTN_HEADS, tq, LANES), F32)]
        + FOX_SCRATCH + [pltpu.VMEM((BRANCH_W, LANES), F32)],
    )
    row3 = lambda a: a.reshape(nseq, 1, BRANCH_W)
    o_p, ob, idx = pl.pallas_call(
        functools.partial(_attn_dec_kernel, tq=tq, tk=tk, npg=npg, nc=nc, nblk=nblk,
                          n_pairs=n_pairs),
        grid_spec=grid_spec,
        out_shape=[jax.ShapeDtypeStruct((t, n_heads * HEAD_DIM), BF16),
                   jax.ShapeDtypeStruct((nseq, 1, BRANCH_W), F32),
                   jax.ShapeDtypeStruct((nseq, N_HEADS, LANES), jnp.int32)],
        compiler_params=_cparams(("arbitrary", "arbitrary")),
        name="attn_dec",
    )(qi, kj, pt_flat, qaug, kaug, vaug,
      row3(qb_s), row3(kb_s), row3(vb_s), lf_s.reshape(nseq, N_HEADS, 1), row3(qa_s),
      *([pool_bk] * npg), *([pool_bv] * npg), *([pool_lft] * npg), *([pool_ak] * npg))
    return o_p, ob.reshape(nseq, BRANCH_W), idx


MOBA_DEC_HEADS = 4


def _moba_dec_kernel(phys_ref, q_ref, kn_ref, vn_ref, *refs, nsel):
    n_kv = MOBA_DEC_HEADS * nsel
    k_refs = refs[:n_kv]
    v_refs = refs[n_kv:2 * n_kv]
    o_ref = refs[2 * n_kv]
    g = pl.program_id(1)
    for hh in range(MOBA_DEC_HEADS):
        h = g * MOBA_DEC_HEADS + hh
        q = q_ref[0, pl.ds(h, 1), :] * QK_SCALE
        q8 = jnp.broadcast_to(q, (8, HEAD_DIM)).astype(BF16)
        s_new = jnp.sum(q * kn_ref[0, pl.ds(h, 1), :], axis=1, keepdims=True)
        scores = [_dot(q8, k_refs[hh * nsel + i][0, 0].astype(BF16))
                  for i in range(nsel)]
        s_all = jnp.concatenate(scores, axis=1)
        m = jnp.maximum(jnp.max(s_all, axis=1, keepdims=True), s_new)
        p_all = jnp.exp(s_all - m)
        p_new = jnp.exp(s_new - m)
        l = jnp.sum(p_all, axis=1, keepdims=True) + p_new
        acc = p_new * vn_ref[0, pl.ds(h, 1), :]
        for i in range(nsel):
            p = p_all[:, i * PAGE:(i + 1) * PAGE].astype(BF16)
            acc = acc + _dot_nt(p, v_refs[hh * nsel + i][0, 0].astype(BF16))
        o_ref[0, pl.ds(h, 1), :] = (acc / l)[0:1, :]


def _moba_dec(page_table, idx, q, k_new, v_new, pool_k, pool_v):
    nseq, n_pages = page_table.shape
    ppb = MOBA_BLOCK // PAGE
    nsel = MOBA_TOPK * ppb
    logical = idx[:, :, :MOBA_TOPK, None] * ppb + jnp.arange(ppb, dtype=jnp.int32)
    phys = page_table[jnp.arange(nseq)[:, None, None, None], logical].reshape(-1)

    def page_map(hh, k, j):
        def f(b, g, ph):
            h = g * MOBA_DEC_HEADS + hh
            return (ph[(b * N_HEADS + h) * nsel + k * ppb + j], h, 0, 0)
        return f

    seq_spec = pl.BlockSpec((1, N_HEADS, HEAD_DIM), lambda b, g, ph: (b, 0, 0))
    page_blk = (1, 1, HEAD_DIM, PAGE)
    sel = [(hh, k, j) for hh in range(MOBA_DEC_HEADS) for k in range(MOBA_TOPK) for j in range(ppb)]
    page_specs = [pl.BlockSpec(page_blk, page_map(hh, k, j)) for hh, k, j in sel]
    grid_spec = pltpu.PrefetchScalarGridSpec(
        num_scalar_prefetch=1,
        grid=(nseq, N_HEADS // MOBA_DEC_HEADS),
        in_specs=[seq_spec, seq_spec, seq_spec] + page_specs + page_specs,
        out_specs=seq_spec,
    )
    per_head = lambda a: a.reshape(nseq, N_HEADS, HEAD_DIM)
    out = pl.pallas_call(
        functools.partial(_moba_dec_kernel, nsel=nsel),
        grid_spec=grid_spec,
        out_shape=jax.ShapeDtypeStruct((nseq, N_HEADS, HEAD_DIM), F32),
        compiler_params=_cparams(("arbitrary", "arbitrary")),
        name="moba_dec",
    )(phys, per_head(q), per_head(k_new), per_head(v_new),
      *([pool_k] * len(sel)), *([pool_v] * len(sel)))
    return out.reshape(nseq, BRANCH_W)


def kernel(x_prompt, x_sample, c_prompt, c_sample, cache_a_k, cache_a_v, cache_b_k, cache_b_v,
           cache_b_logf, page_table, w_ada, b_ada, g_norm1, g_norm2, g_final, w_in, b_f,
           w_a_o, w_b_o, w_out, w_gate, w_up, w_down):
    depth = w_ada.shape[0]
    assert depth == 1
    batch, seq, _ = x_prompt.shape
    nseq, dec_seq, _ = x_sample.shape
    assert batch == 1 and dec_seq == 1
    n_pool = cache_a_k.shape[1]
    n_pages = page_table.shape[1]
    past_len = n_pages * PAGE
    assert past_len % MOBA_BLOCK == 0

    w_in0 = w_in[0]
    n_qkv = 6 * BRANCH_W
    wqkv = w_in0[:, :n_qkv].astype(BF16)
    wf = jnp.pad(w_in0[:, n_qkv:n_qkv + N_HEADS], ((0, 0), (0, LANES - N_HEADS))).astype(BF16)
    wg = w_in0[:, n_qkv + N_HEADS:].astype(BF16)
    bf = jnp.pad(b_f[0], (0, LANES - N_HEADS)).reshape(1, LANES)
    wao = w_a_o[0].astype(BF16)
    wbo = w_b_o[0].astype(BF16)
    wout = w_out[0].astype(BF16)
    wgate = w_gate[0].astype(BF16)
    wup = w_up[0].astype(BF16)
    wdown = w_down[0].astype(BF16)
    g1 = g_norm1[0].reshape(1, D_MODEL)
    g2 = g_norm2[0].reshape(1, D_MODEL)
    gf = g_final.reshape(1, D_MODEL)
    inv8 = ROPE_THETA ** (-jnp.arange(ROT_HALF, dtype=F32) / ROT_HALF)
    lane = np.arange(LANES)
    inv = jnp.where(jnp.asarray((lane % HEAD_DIM) < ROT_DIM), inv8[lane % ROT_HALF], 0.0)
    inv = inv.reshape(1, LANES).astype(F32)

    pad = (-(batch + nseq)) % 8
    c_all = jnp.concatenate([c_prompt, c_sample, jnp.zeros((pad, D_MODEL), F32)], axis=0)
    mods = _ada(c_all, w_ada[0], b_ada[0])
    mp = [mods[0:1, j * D_MODEL:(j + 1) * D_MODEL] for j in range(6)]
    ms = [mods[batch:batch + nseq, j * D_MODEL:(j + 1) * D_MODEL] for j in range(6)]

    xp = x_prompt.reshape(seq, D_MODEL)
    qa, ka, va, qb, kb, vb, lf, sga, sgb = _inproj(
        xp, mp[0], mp[1], g1, wqkv, wf, wg, bf, inv, tm=256, pos0=0, pos_stride=1)
    xs = x_sample.reshape(nseq, D_MODEL)
    qa_s, ka_s, va_s, qb_s, kb_s, vb_s, lf_s, sga_s, sgb_s = _inproj(
        xs, ms[0], ms[1], g1, wqkv, wf, wg, bf, inv, tm=nseq, pos0=past_len, pos_stride=0)

    pool_t = lambda c: jnp.transpose(c[0], (0, 2, 3, 1))
    pool_ak, pool_av = pool_t(cache_a_k), pool_t(cache_a_v)
    pool_bk, pool_bv = pool_t(cache_b_k), pool_t(cache_b_v)
    pool_lft = jnp.transpose(cache_b_logf[0], (0, 2, 1))

    qaug, kaug, vaug = _prep(qa, ka, va, qb, kb, vb, lf)
    o_p, ob_s, idx = _attn_dec(qaug, kaug, vaug, page_table, qb_s, kb_s, vb_s,
                               lf_s[:, :N_HEADS], qa_s, pool_bk, pool_bv, pool_lft, pool_ak,
                               tq=1024, npg=16)
    y_p = _post(xp, o_p, sga, sgb, mp[2], mp[3], mp[4], mp[5], g2, gf,
                wao, wbo, wout, wgate, wup, wdown, tm=256)

    oa_s = _moba_dec(page_table, idx, qa_s, ka_s, va_s, pool_ak, pool_av)
    o_s = jnp.concatenate([oa_s, ob_s], axis=1)
    y_s = _post(xs, o_s, sga_s, sgb_s, ms[2], ms[3], ms[4], ms[5], g2, gf,
                wao, wbo, wout, wgate, wup, wdown, tm=nseq)

    return (y_p.reshape(batch, seq, D_MODEL),
            y_s.reshape(nseq, dec_seq, D_MODEL),
            ka.reshape(depth, batch, seq, N_HEADS, HEAD_DIM),
            va.reshape(depth, batch, seq, N_HEADS, HEAD_DIM),
            kb.reshape(depth, batch, seq, N_HEADS, HEAD_DIM),
            vb.reshape(depth, batch, seq, N_HEADS, HEAD_DIM),
            lf[:, :N_HEADS].reshape(depth, batch, seq, N_HEADS),
            ka_s.reshape(depth, nseq, dec_seq, N_HEADS, HEAD_DIM),
            va_s.reshape(depth, nseq, dec_seq, N_HEADS, HEAD_DIM),
            kb_s.reshape(depth, nseq, dec_seq, N_HEADS, HEAD_DIM),
            vb_s.reshape(depth, nseq, dec_seq, N_HEADS, HEAD_DIM),
            lf_s[:, :N_HEADS].reshape(depth, nseq, dec_seq, N_HEADS))
```

```python
import functools

import numpy as np
import jax
import jax.numpy as jnp
from jax import lax
from jax.experimental import pallas as pl
from jax.experimental.pallas import tpu as pltpu

F32 = jnp.float32
BF16 = jnp.bfloat16

D_MODEL = 1024
HEAD_DIM = 64
N_HEADS = 8
BRANCH_W = N_HEADS * HEAD_DIM
D_FF = 2816
MOBA_BLOCK = 256
MOBA_TOPK = 3
ROT_DIM = 16
ROT_HALF = 8
ROPE_THETA = 500000.0
RMS_EPS = 1e-6
NEG_INF = -1e30
PAGE = 128
QK_SCALE = HEAD_DIM ** -0.5
LOG2E = 1.4426950408889634
LANES = 128
AUG_W = 128

VMEM_LIMIT = 56 * 1024 * 1024


def _cparams(sem, vmem=VMEM_LIMIT):
    return pltpu.CompilerParams(dimension_semantics=sem, vmem_limit_bytes=vmem)


def _split2(x):
    hi = x.astype(BF16)
    lo = (x - hi.astype(F32)).astype(BF16)
    return hi, lo


def _split3(x):
    hi = x.astype(BF16)
    r = x - hi.astype(F32)
    mid = r.astype(BF16)
    lo = (r - mid.astype(F32)).astype(BF16)
    return hi, mid, lo


def _dot(a, b):
    return jnp.dot(a, b, preferred_element_type=F32)


def _dot_nt(a, b):
    return lax.dot_general(a, b, (((1,), (1,)), ((), ())), preferred_element_type=F32)


def _dot3(a, b):
    ah, al = _split2(a)
    bh, bl = _split2(b)
    return _dot(ah, bh) + _dot(ah, bl) + _dot(al, bh)


def _dot3_nt(a, b):
    ah, al = _split2(a)
    bh, bl = _split2(b)
    return _dot_nt(ah, bh) + _dot_nt(ah, bl) + _dot_nt(al, bh)


def _rms(x, g):
    return x * lax.rsqrt(jnp.mean(x * x, axis=-1, keepdims=True) + RMS_EPS) * g


def _sigmoid(x):
    return 1.0 / (1.0 + jnp.exp(-x))


def _ada_kernel(c_ref, w_ref, b_ref, o_ref):
    c = c_ref[...]
    a = c * _sigmoid(c)
    o_ref[...] = _dot3(a, w_ref[...]) + b_ref[...]


def _ada(c_all, w_ada, b_ada):
    rows = c_all.shape[0]
    n = w_ada.shape[1]
    tn = 1536
    return pl.pallas_call(
        _ada_kernel,
        grid=(n // tn,),
        in_specs=[pl.BlockSpec((rows, D_MODEL), lambda j: (0, 0)),
                  pl.BlockSpec((D_MODEL, tn), lambda j: (0, j)),
                  pl.BlockSpec((1, tn), lambda j: (0, j))],
        out_specs=pl.BlockSpec((rows, tn), lambda j: (0, j)),
        out_shape=jax.ShapeDtypeStruct((rows, n), F32),
        compiler_params=_cparams(("arbitrary",)),
        name="ada",
    )(c_all, w_ada, b_ada.reshape(1, n))


def _inproj_kernel(x_ref, sh_ref, sc_ref, g_ref, wqkv_ref, wf_ref, wg_ref, bf_ref, inv_ref,
                   qa_ref, ka_ref, va_ref, qb_ref, kb_ref, vb_ref, lf_ref, sga_ref, sgb_ref,
                   *, tm, pos0, pos_stride):
    i = pl.program_id(0)
    x = x_ref[...]
    u = _rms(x, g_ref[...]) * (1.0 + sc_ref[...]) + sh_ref[...]
    ub = u.astype(BF16)

    row = lax.broadcasted_iota(jnp.int32, (tm, 1), 0) + i * tm
    pos = (pos0 + pos_stride * row).astype(F32)
    ang = pos * inv_ref[...]
    cos = jnp.cos(ang)
    sin = jnp.sin(ang)
    lane = lax.broadcasted_iota(jnp.int32, (tm, LANES), 1)
    first_half = (lane % HEAD_DIM) < ROT_HALF

    def rope(y):
        parts = []
        for g in range(BRANCH_W // LANES):
            yg = y[:, g * LANES:(g + 1) * LANES]
            partner = jnp.where(first_half,
                                -pltpu.roll(yg, LANES - ROT_HALF, axis=1),
                                pltpu.roll(yg, ROT_HALF, axis=1))
            parts.append(yg * cos + partner * sin)
        return jnp.concatenate(parts, axis=1)

    def proj(j):
        return _dot(ub, wqkv_ref[:, j * BRANCH_W:(j + 1) * BRANCH_W])

    qa_ref[...] = rope(proj(0))
    ka_ref[...] = rope(proj(1))
    va_ref[...] = proj(2)
    qb_ref[...] = proj(3)
    kb_ref[...] = proj(4)
    vb_ref[...] = proj(5)

    z = _dot(ub, wf_ref[...]) + bf_ref[...]
    lf_ref[...] = jnp.minimum(z, 0.0) - jnp.log(1.0 + jnp.exp(-jnp.abs(z)))

    sga_ref[...] = _sigmoid(_dot(ub, wg_ref[:, :D_MODEL]))
    sgb_ref[...] = _sigmoid(_dot(ub, wg_ref[:, D_MODEL:]))


def _inproj(x, sh, sc, g1, wqkv, wf, wg, bf, inv, *, tm, pos0, pos_stride):
    rows = x.shape[0]
    per_row = sh.shape[0] != 1
    mod_spec = (pl.BlockSpec((tm, D_MODEL), lambda i: (i, 0)) if per_row
                else pl.BlockSpec((1, D_MODEL), lambda i: (0, 0)))
    const = lambda shape: pl.BlockSpec(shape, lambda i: (0, 0), pipeline_mode=pl.Buffered(1))
    row_spec = lambda w: pl.BlockSpec((tm, w), lambda i: (i, 0))
    out_w = [BRANCH_W] * 6 + [LANES, D_MODEL, D_MODEL]
    return pl.pallas_call(
        functools.partial(_inproj_kernel, tm=tm, pos0=pos0, pos_stride=pos_stride),
        grid=(rows // tm,),
        in_specs=[row_spec(D_MODEL), mod_spec, mod_spec, const((1, D_MODEL)),
                  const(wqkv.shape), const(wf.shape), const(wg.shape),
                  const((1, LANES)), const((1, LANES))],
        out_specs=[row_spec(w) for w in out_w],
        out_shape=[jax.ShapeDtypeStruct((rows, w), F32) for w in out_w],
        compiler_params=_cparams(("arbitrary",)),
        name="inproj",
    )(x, sh, sc, g1, wqkv, wf, wg, bf, inv)


def _top3_mask(s, n_valid):
    nblk, rows = s.shape
    blk = lax.broadcasted_iota(jnp.int32, (nblk, rows), 0).astype(F32)
    n_valid = n_valid.astype(F32)
    s = jnp.where(blk < n_valid, s, NEG_INF)
    sel = jnp.zeros((nblk, rows), F32)
    for k in range(MOBA_TOPK):
        m = jnp.max(s, axis=0, keepdims=True)
        first = jnp.min(jnp.where(s == m, blk, float(nblk)), axis=0, keepdims=True)
        pick = blk == first
        sel = jnp.where(pick & (n_valid > k), 1.0, sel)
        s = jnp.where(pick, -3e38, s)
    return sel > 0.5


def _prep_kernel(qa_ref, ka_ref, va_ref, qb_ref, kb_ref, vb_ref, lf_ref,
                 qaug_ref, kaug_ref, vaug_ref, stats_ref, km_ref, fc_ref, *, tm, nblk):
    i = pl.program_id(0)

    @pl.when(i == 0)
    def _():
        km_ref[...] = jnp.zeros_like(km_ref)
        fc_ref[...] = jnp.zeros_like(fc_ref)

    lf = lf_ref[...]
    r_i = lax.broadcasted_iota(jnp.int32, (tm, tm), 0)
    c_i = lax.broadcasted_iota(jnp.int32, (tm, tm), 1)
    tri = jnp.where(c_i <= r_i, 1.0, 0.0).astype(BF16)
    l_hi, l_mid, l_lo = _split3(lf)
    fcum = _dot(tri, l_hi) + _dot(tri, l_mid) + _dot(tri, l_lo) + fc_ref[...]
    fc_ref[...] = fcum[tm - 1:tm, :]
    f_hi, f_mid, f_lo = [p.astype(F32) for p in _split3(-fcum * LOG2E)]

    qa = qa_ref[...]
    ka = ka_ref[...]
    va = va_ref[...]
    qb = qb_ref[...]
    kb = kb_ref[...]
    vb = vb_ref[...]
    km = km_ref[...]

    lane64 = lax.broadcasted_iota(jnp.int32, (tm, HEAD_DIM), 1)
    own = lane64 == i
    own_f = jnp.where(own, 1.0, 0.0)
    ones3 = jnp.where(lane64 < 3, 1.0, 0.0)
    ones1 = jnp.where(lane64 < 1, 1.0, 0.0)

    def put(ref, g, left, right):
        ref[:, g * AUG_W:(g + 1) * AUG_W] = jnp.concatenate([left, right], axis=1).astype(BF16)

    own_t = lax.broadcasted_iota(jnp.int32, (HEAD_DIM, tm), 0) == i

    for h in range(N_HEADS):
        hs = slice(h * HEAD_DIM, (h + 1) * HEAD_DIM)
        qh = qa[:, hs]
        s_t = _dot3_nt(km[:, hs], qh)
        sel_t = _top3_mask(s_t, i)
        bias = jnp.where(sel_t | own_t, 0.0, NEG_INF).T
        put(qaug_ref, h, qh * (QK_SCALE * LOG2E), bias)
        put(kaug_ref, h, ka[:, hs], own_f)
        put(vaug_ref, h, va[:, hs], ones1)

        g = N_HEADS + h
        put(qaug_ref, g, qb[:, hs] * (QK_SCALE * LOG2E), ones3)
        dec = jnp.where(lane64 == 0, f_hi[:, h:h + 1],
                        jnp.where(lane64 == 1, f_mid[:, h:h + 1],
                                  jnp.where(lane64 == 2, f_lo[:, h:h + 1], 0.0)))
        put(kaug_ref, g, kb[:, hs], dec)
        put(vaug_ref, g, vb[:, hs], ones1)

    km_ref[pl.ds(i, 1), :] = jnp.mean(ka, axis=0, keepdims=True)

    e_r = lax.broadcasted_iota(jnp.int32, (BRANCH_W, LANES), 0)
    e_c = lax.broadcasted_iota(jnp.int32, (BRANCH_W, LANES), 1)
    head_sum = jnp.where(e_r // HEAD_DIM == e_c, 1.0, 0.0).astype(BF16)
    qn2 = jnp.max(_dot((qb * qb).astype(BF16), head_sum), axis=0, keepdims=True)
    kn2 = jnp.max(_dot((kb * kb).astype(BF16), head_sum), axis=0, keepdims=True)
    stats_ref[0] = jnp.concatenate([qn2, kn2, fcum[0:1, :], fcum[tm - 1:tm, :]], axis=0)


def _prep(qa, ka, va, qb, kb, vb, lf):
    t = qa.shape[0]
    tm = MOBA_BLOCK
    nblk = t // tm
    assert nblk <= HEAD_DIM, "block-selection bias lanes hold at most 64 blocks"
    row = lambda w: pl.BlockSpec((tm, w), lambda i: (i, 0))
    n_aug = 2 * N_HEADS * AUG_W
    return pl.pallas_call(
        functools.partial(_prep_kernel, tm=tm, nblk=nblk),
        grid=(nblk,),
        in_specs=[row(BRANCH_W)] * 6 + [row(LANES)],
        out_specs=[row(n_aug)] * 3 + [pl.BlockSpec((1, 4, LANES), lambda i: (i, 0, 0))],
        out_shape=[jax.ShapeDtypeStruct((t, n_aug), BF16)] * 3
        + [jax.ShapeDtypeStruct((nblk, 4, LANES), F32)],
        scratch_shapes=[pltpu.VMEM((HEAD_DIM, BRANCH_W), F32),
                        pltpu.VMEM((1, LANES), F32)],
        compiler_params=_cparams(("arbitrary",)),
        name="prep",
    )(qa, ka, va, qb, kb, vb, lf)


ATTN_HEADS = 4


def _attn_kernel(qi_ref, kj_ref, cnt_ref, q_ref, k_ref, v_ref, o_ref, m_ref, acc_ref,
                 *, tq, tk, n_pairs):
    g = pl.program_id(0)
    p_id = pl.program_id(1)
    base = g * n_pairs
    qi = qi_ref[base + p_id]
    kj = kj_ref[base + p_id]
    active = p_id < cnt_ref[g]
    first = (p_id == 0) | (qi != qi_ref[base + jnp.maximum(p_id - 1, 0)])

    @pl.when(active & first)
    def _():
        m_ref[...] = jnp.full_like(m_ref, -jnp.inf)
        acc_ref[...] = jnp.zeros_like(acc_ref)

    def step(diag):
        for hh in range(ATTN_HEADS):
            hs = slice(hh * AUG_W, (hh + 1) * AUG_W)
            s = _dot_nt(q_ref[:, hs], k_ref[:, hs])
            if diag:
                r = lax.broadcasted_iota(jnp.int32, (tq, tk), 0)
                c = lax.broadcasted_iota(jnp.int32, (tq, tk), 1)
                s = jnp.where(c <= r, s, NEG_INF)
            m_prev = m_ref[hh]
            m_new = jnp.maximum(m_prev, jnp.max(s, axis=1, keepdims=True))
            alpha = jnp.exp2(m_prev - m_new)
            p = jnp.concatenate(
                [jnp.exp2(s[:, j * LANES:(j + 1) * LANES] - m_new).astype(BF16)
                 for j in range(tk // LANES)], axis=1)
            acc_ref[hh] = alpha * acc_ref[hh] + _dot(p, v_ref[:, hs])
            m_ref[hh] = m_new

    @pl.when(active & (kj < qi))
    def _():
        step(False)

    @pl.when(active & (kj == qi))
    def _():
        step(True)
        outs = []
        for hh in range(ATTN_HEADS):
            acc = acc_ref[hh]
            outs.append(acc[:, :HEAD_DIM] / acc[:, HEAD_DIM:HEAD_DIM + 1])
        o_ref[...] = jnp.concatenate(outs, axis=1).astype(o_ref.dtype)


SKIP_LOG2 = -160.0


def _attn_tile_lists(nq, n_groups, stats):
    qn2, kn2, f_first, f_last = stats
    per_tile = lambda a, fn: fn(a.reshape(nq, -1, N_HEADS), axis=1)
    qmax = jnp.sqrt(per_tile(qn2, jnp.max)) * 1.01
    kmax = jnp.sqrt(per_tile(kn2, jnp.max)) * 1.01
    ff = f_first.reshape(nq, -1, N_HEADS)[:, 0, :]
    fl = f_last.reshape(nq, -1, N_HEADS)[:, -1, :]
    c2 = QK_SCALE * LOG2E
    bound = (qmax[:, None, :] * (kmax[None, :, :] + kmax[:, None, :]) * c2
             + (ff[:, None, :] - fl[None, :, :]) * LOG2E)
    need_fox = bound > SKIP_LOG2
    ii = np.arange(nq)[:, None]
    jj = np.arange(nq)[None, :]
    causal = jnp.asarray(jj <= ii)
    diag = jnp.asarray(jj == ii)
    heads_per_branch_group = N_HEADS // (n_groups // 2)
    n_pairs = nq * (nq + 1) // 2
    qi_all, kj_all, cnt_all = [], [], []
    for g in range(n_groups):
        if g < n_groups // 2:
            need = causal
        else:
            h0 = (g - n_groups // 2) * heads_per_branch_group
            need = (jnp.any(need_fox[:, :, h0:h0 + heads_per_branch_group], axis=-1) | diag) & causal
        flat = need.reshape(-1)
        pos = jnp.nonzero(flat, size=n_pairs, fill_value=nq * nq - 1)[0]
        cnt = jnp.sum(flat).astype(jnp.int32)
        last = pos[jnp.maximum(cnt - 1, 0)]
        pos = jnp.where(jnp.arange(n_pairs) < cnt, pos, last)
        qi_all.append((pos // nq).astype(jnp.int32))
        kj_all.append((pos % nq).astype(jnp.int32))
        cnt_all.append(cnt)
    return jnp.concatenate(qi_all), jnp.concatenate(kj_all), jnp.stack(cnt_all), n_pairs


def _attn(qaug, kaug, v, stats, *, tq):
    t = qaug.shape[0]
    tk = tq
    nq = t // tq
    n_heads = qaug.shape[1] // AUG_W
    n_groups = n_heads // ATTN_HEADS
    qi, kj, cnt, n_pairs = _attn_tile_lists(nq, n_groups, stats)
    w_in = ATTN_HEADS * AUG_W
    w_out = ATTN_HEADS * HEAD_DIM
    grid_spec = pltpu.PrefetchScalarGridSpec(
        num_scalar_prefetch=3,
        grid=(n_groups, n_pairs),
        in_specs=[pl.BlockSpec((tq, w_in), lambda h, p, qi, kj, cnt: (qi[h * n_pairs + p], h)),
                  pl.BlockSpec((tk, w_in), lambda h, p, qi, kj, cnt: (kj[h * n_pairs + p], h)),
                  pl.BlockSpec((tk, w_in), lambda h, p, qi, kj, cnt: (kj[h * n_pairs + p], h))],
        out_specs=pl.BlockSpec((tq, w_out), lambda h, p, qi, kj, cnt: (qi[h * n_pairs + p], h)),
        scratch_shapes=[pltpu.VMEM((ATTN_HEADS, tq, LANES), F32),
                        pltpu.VMEM((ATTN_HEADS, tq, LANES), F32)],
    )
    return pl.pallas_call(
        functools.partial(_attn_kernel, tq=tq, tk=tk, n_pairs=n_pairs),
        grid_spec=grid_spec,
        out_shape=jax.ShapeDtypeStruct((t, n_heads * HEAD_DIM), BF16),
        compiler_params=_cparams(("arbitrary", "arbitrary")),
        name="attn",
    )(qi, kj, cnt, qaug, kaug, v)


def _post_kernel(x_ref, o_ref, sga_ref, sgb_ref, gt1_ref, sh2_ref, sc2_ref, gt2_ref,
                 g2_ref, gf_ref, wao_ref, wbo_ref, wout_ref, wgate_ref, wup_ref, wdown_ref,
                 y_ref):
    o = o_ref[...].astype(BF16)
    ma = _dot(o[:, :BRANCH_W], wao_ref[...])
    mb = _dot(o[:, BRANCH_W:], wbo_ref[...])
    mix = (sga_ref[...] * ma + sgb_ref[...] * mb).astype(BF16)
    h1 = x_ref[...] + gt1_ref[...] * _dot(mix, wout_ref[...])
    u2 = (_rms(h1, g2_ref[...]) * (1.0 + sc2_ref[...]) + sh2_ref[...]).astype(BF16)
    gate = _dot(u2, wgate_ref[...])
    up = _dot(u2, wup_ref[...])
    act = (gate * _sigmoid(gate) * up).astype(BF16)
    h2 = h1 + gt2_ref[...] * _dot(act, wdown_ref[...])
    y_ref[...] = _rms(h2, gf_ref[...])


def _post(x, o, sga, sgb, gt1, sh2, sc2, gt2, g2, gf, wao, wbo, wout, wgate, wup, wdown, *, tm):
    rows = x.shape[0]
    per_row = gt1.shape[0] != 1
    mod_spec = (pl.BlockSpec((tm, D_MODEL), lambda i: (i, 0)) if per_row
                else pl.BlockSpec((1, D_MODEL), lambda i: (0, 0)))
    const = lambda a: pl.BlockSpec(a.shape, lambda i: (0, 0), pipeline_mode=pl.Buffered(1))
    row = pl.BlockSpec((tm, D_MODEL), lambda i: (i, 0))
    return pl.pallas_call(
        _post_kernel,
        grid=(rows // tm,),
        in_specs=[row, row, row, row, mod_spec, mod_spec, mod_spec, mod_spec,
                  const(g2), const(gf), const(wao), const(wbo), const(wout),
                  const(wgate), const(wup), const(wdown)],
        out_specs=row,
        out_shape=jax.ShapeDtypeStruct((rows, D_MODEL), F32),
        compiler_params=_cparams(("arbitrary",)),
        name="post",
    )(x, o, sga, sgb, gt1, sh2, sc2, gt2, g2, gf, wao, wbo, wout, wgate, wup, wdown)


def _head_mask(rows=N_HEADS, width=BRANCH_W):
    r = lax.broadcasted_iota(jnp.int32, (rows, width), 0)
    c = lax.broadcasted_iota(jnp.int32, (rows, width), 1)
    return (c // HEAD_DIM) == r


def _fox_qbd(q_ref):
    return jnp.where(_head_mask(), q_ref[0] * QK_SCALE, 0.0)


def _fox_init(c, q_ref, kn_ref, vn_ref, m_ref, l_ref, acc_ref, car_ref):
    @pl.when(c == 0)
    def _():
        m_ref[...] = jnp.sum(_fox_qbd(q_ref) * kn_ref[0], axis=1, keepdims=True)
        l_ref[...] = jnp.ones_like(l_ref)
        acc_ref[...] = jnp.broadcast_to(vn_ref[0], acc_ref.shape)
        car_ref[...] = jnp.zeros_like(car_ref)


def _fox_final(c, nc, o_ref, l_ref, acc_ref):
    @pl.when(c == nc - 1)
    def _():
        o = jnp.where(_head_mask(), acc_ref[...] / l_ref[...], 0.0)
        o_ref[0] = jnp.sum(o, axis=0, keepdims=True)


def _fox_main(q_ref, lfn_ref, k_refs, v_refs, lf_refs, m_ref, l_ref, acc_ref, car_ref):
    npg = len(k_refs)
    qb16 = _fox_qbd(q_ref).astype(BF16)
    r_i = lax.broadcasted_iota(jnp.int32, (PAGE, PAGE), 0)
    c_i = lax.broadcasted_iota(jnp.int32, (PAGE, PAGE), 1)
    later = jnp.where(r_i > c_i, 1.0, 0.0).astype(BF16)
    lfn = lfn_ref[0]

    carry = car_ref[...]
    scores = []
    for i in range(npg):
        lft = lf_refs[i][0]
        h3, m3, l3 = _split3(lft)
        suffix = _dot(h3, later) + _dot(m3, later) + _dot(l3, later) + carry
        carry = carry + jnp.sum(lft, axis=1, keepdims=True)
        kt = k_refs[i][0].reshape(BRANCH_W, PAGE).astype(BF16)
        s = _dot(qb16, kt)
        scores.append(s + suffix + lfn)
    car_ref[...] = carry

    s_all = jnp.concatenate(scores, axis=1)
    m_prev = m_ref[...]
    m_new = jnp.maximum(m_prev, jnp.max(s_all, axis=1, keepdims=True))
    alpha = jnp.exp(m_prev - m_new)
    p_all = jnp.exp(s_all - m_new)
    l_ref[...] = alpha * l_ref[...] + jnp.sum(p_all, axis=1, keepdims=True)
    acc = alpha * acc_ref[...]
    for i in range(npg):
        p = p_all[:, i * PAGE:(i + 1) * PAGE].astype(BF16)
        vt = v_refs[i][0].reshape(BRANCH_W, PAGE).astype(BF16)
        acc = acc + _dot_nt(p, vt)
    acc_ref[...] = acc
    m_ref[...] = m_new


FOX_SCRATCH = [pltpu.VMEM((N_HEADS, 1), F32), pltpu.VMEM((N_HEADS, 1), F32),
               pltpu.VMEM((N_HEADS, BRANCH_W), F32), pltpu.VMEM((N_HEADS, 1), F32)]


def _sel_init(c, km_ref):
    @pl.when(c == 0)
    def _():
        km_ref[...] = jnp.zeros_like(km_ref)


def _sel_main(c, k_refs, km_ref):
    npg = len(k_refs)
    ppb = MOBA_BLOCK // PAGE
    lane = lax.broadcasted_iota(jnp.int32, (BRANCH_W, LANES), 1)
    km = km_ref[...]
    for n in range(npg // ppb):
        tot = k_refs[n * ppb][0].reshape(BRANCH_W, PAGE)
        for j in range(1, ppb):
            tot = tot + k_refs[n * ppb + j][0].reshape(BRANCH_W, PAGE)
        mean = jnp.sum(tot, axis=1, keepdims=True) * (1.0 / MOBA_BLOCK)
        km = jnp.where(lane == c * (npg // ppb) + n, mean, km)
    km_ref[...] = km


def _sel_final(c, nc, nblk, q_ref, idx_ref, km_ref):
    @pl.when(c == nc - 1)
    def _():
        qbd = jnp.where(_head_mask(), q_ref[0], 0.0)
        s = _dot3(qbd, km_ref[...])
        col = lax.broadcasted_iota(jnp.int32, s.shape, 1).astype(F32)
        s = jnp.where(col < nblk, s, NEG_INF)
        out = jnp.zeros(s.shape, F32)
        for k in range(MOBA_TOPK):
            m = jnp.max(s, axis=1, keepdims=True)
            first = jnp.min(jnp.where(s == m, col, float(LANES)), axis=1, keepdims=True)
            out = jnp.where(col == k, first, out)
            s = jnp.where(col == first, -3e38, s)
        idx_ref[0] = out.astype(jnp.int32)


def _fox_dec_kernel(pt_ref, q_ref, kn_ref, vn_ref, lfn_ref, *refs, npg):
    k_refs = refs[:npg]
    v_refs = refs[npg:2 * npg]
    lf_refs = refs[2 * npg:3 * npg]
    o_ref = refs[3 * npg]
    m_ref, l_ref, acc_ref, car_ref = refs[3 * npg + 1:]
    c = pl.program_id(1)
    _fox_init(c, q_ref, kn_ref, vn_ref, m_ref, l_ref, acc_ref, car_ref)
    _fox_main(q_ref, lfn_ref, k_refs, v_refs, lf_refs, m_ref, l_ref, acc_ref, car_ref)
    _fox_final(c, pl.num_programs(1), o_ref, l_ref, acc_ref)


def _fox_dec(page_table, q, k_new, v_new, lf_new, pool_k, pool_v, pool_lft, *, npg):
    nseq, n_pages = page_table.shape
    last = n_pages - 1
    pt_flat = page_table.reshape(-1)

    def page_map(i, nd):
        return lambda b, c, pt: (pt[b * n_pages + (last - (c * npg + i))],) + (0,) * (nd - 1)

    seq3 = lambda w: pl.BlockSpec((1, 1, w), lambda b, c, pt: (b, 0, 0))
    page_blk = (1, N_HEADS, HEAD_DIM, PAGE)
    in_specs = [seq3(BRANCH_W), seq3(BRANCH_W), seq3(BRANCH_W),
                pl.BlockSpec((1, N_HEADS, 1), lambda b, c, pt: (b, 0, 0))]
    in_specs += [pl.BlockSpec(page_blk, page_map(i, 4)) for i in range(npg)]
    in_specs += [pl.BlockSpec(page_blk, page_map(i, 4)) for i in range(npg)]
    in_specs += [pl.BlockSpec((1, N_HEADS, PAGE), page_map(i, 3)) for i in range(npg)]
    grid_spec = pltpu.PrefetchScalarGridSpec(
        num_scalar_prefetch=1,
        grid=(nseq, n_pages // npg),
        in_specs=in_specs,
        out_specs=seq3(BRANCH_W),
        scratch_shapes=FOX_SCRATCH,
    )
    row3 = lambda a: a.reshape(nseq, 1, BRANCH_W)
    out = pl.pallas_call(
        functools.partial(_fox_dec_kernel, npg=npg),
        grid_spec=grid_spec,
        out_shape=jax.ShapeDtypeStruct((nseq, 1, BRANCH_W), F32),
        compiler_params=_cparams(("arbitrary", "arbitrary")),
        name="fox_dec",
    )(pt_flat, row3(q), row3(k_new), row3(v_new), lf_new.reshape(nseq, N_HEADS, 1),
      *([pool_k] * npg), *([pool_v] * npg), *([pool_lft] * npg))
    return out.reshape(nseq, BRANCH_W)


def _moba_sel_kernel(pt_ref, q_ref, *refs, npg, nblk):
    k_refs = refs[:npg]
    idx_ref = refs[npg]
    km_ref = refs[npg + 1]
    c = pl.program_id(1)
    _sel_init(c, km_ref)
    _sel_main(c, k_refs, km_ref)
    _sel_final(c, pl.num_programs(1), nblk, q_ref, idx_ref, km_ref)


def _moba_sel(page_table, q, pool_k, *, npg):
    nseq, n_pages = page_table.shape
    nblk = n_pages * PAGE // MOBA_BLOCK
    assert nblk >= MOBA_TOPK and nblk <= LANES
    pt_flat = page_table.reshape(-1)

    def page_map(i):
        return lambda b, c, pt: (pt[b * n_pages + c * npg + i], 0, 0, 0)

    grid_spec = pltpu.PrefetchScalarGridSpec(
        num_scalar_prefetch=1,
        grid=(nseq, n_pages // npg),
        in_specs=[pl.BlockSpec((1, 1, BRANCH_W), lambda b, c, pt: (b, 0, 0))]
        + [pl.BlockSpec((1, N_HEADS, HEAD_DIM, PAGE), page_map(i)) for i in range(npg)],
        out_specs=pl.BlockSpec((1, N_HEADS, LANES), lambda b, c, pt: (b, 0, 0)),
        scratch_shapes=[pltpu.VMEM((BRANCH_W, LANES), F32)],
    )
    return pl.pallas_call(
        functools.partial(_moba_sel_kernel, npg=npg, nblk=nblk),
        grid_spec=grid_spec,
        out_shape=jax.ShapeDtypeStruct((nseq, N_HEADS, LANES), jnp.int32),
        compiler_params=_cparams(("arbitrary", "arbitrary")),
        name="moba_sel",
    )(pt_flat, q.reshape(nseq, 1, BRANCH_W), *([pool_k] * npg))


MOBA_DEC_HEADS = 4


def _moba_dec_kernel(phys_ref, q_ref, kn_ref, vn_ref, *refs, nsel):
    n_kv = MOBA_DEC_HEADS * nsel
    k_refs = refs[:n_kv]
    v_refs = refs[n_kv:2 * n_kv]
    o_ref = refs[2 * n_kv]
    g = pl.program_id(1)
    for hh in range(MOBA_DEC_HEADS):
        h = g * MOBA_DEC_HEADS + hh
        q = q_ref[0, pl.ds(h, 1), :] * QK_SCALE
        q8 = jnp.broadcast_to(q, (8, HEAD_DIM)).astype(BF16)
        s_new = jnp.sum(q * kn_ref[0, pl.ds(h, 1), :], axis=1, keepdims=True)
        scores = [_dot(q8, k_refs[hh * nsel + i][0, 0].astype(BF16))
                  for i in range(nsel)]
        s_all = jnp.concatenate(scores, axis=1)
        m = jnp.maximum(jnp.max(s_all, axis=1, keepdims=True), s_new)
        p_all = jnp.exp(s_all - m)
        p_new = jnp.exp(s_new - m)
        l = jnp.sum(p_all, axis=1, keepdims=True) + p_new
        acc = p_new * vn_ref[0, pl.ds(h, 1), :]
        for i in range(nsel):
            p = p_all[:, i * PAGE:(i + 1) * PAGE].astype(BF16)
            acc = acc + _dot_nt(p, v_refs[hh * nsel + i][0, 0].astype(BF16))
        o_ref[0, pl.ds(h, 1), :] = (acc / l)[0:1, :]


def _moba_dec(page_table, idx, q, k_new, v_new, pool_k, pool_v):
    nseq, n_pages = page_table.shape
    ppb = MOBA_BLOCK // PAGE
    nsel = MOBA_TOPK * ppb
    logical = idx[:, :, :MOBA_TOPK, None] * ppb + jnp.arange(ppb, dtype=jnp.int32)
    phys = page_table[jnp.arange(nseq)[:, None, None, None], logical].reshape(-1)

    def page_map(hh, k, j):
        def f(b, g, ph):
            h = g * MOBA_DEC_HEADS + hh
            return (ph[(b * N_HEADS + h) * nsel + k * ppb + j], h, 0, 0)
        return f

    seq_spec = pl.BlockSpec((1, N_HEADS, HEAD_DIM), lambda b, g, ph: (b, 0, 0))
    page_blk = (1, 1, HEAD_DIM, PAGE)
    sel = [(hh, k, j) for hh in range(MOBA_DEC_HEADS) for k in range(MOBA_TOPK) for j in range(ppb)]
    page_specs = [pl.BlockSpec(page_blk, page_map(hh, k, j)) for hh, k, j in sel]
    grid_spec = pltpu.PrefetchScalarGridSpec(
        num_scalar_prefetch=1,
        grid=(nseq, N_HEADS // MOBA_DEC_HEADS),
        in_specs=[seq_spec, seq_spec, seq_spec] + page_specs + page_specs,
        out_specs=seq_spec,
    )
    per_head = lambda a: a.reshape(nseq, N_HEADS, HEAD_DIM)
    out = pl.pallas_call(
        functools.partial(_moba_dec_kernel, nsel=nsel),
        grid_spec=grid_spec,
        out_shape=jax.ShapeDtypeStruct((nseq, N_HEADS, HEAD_DIM), F32),
        compiler_params=_cparams(("arbitrary", "arbitrary")),
        name="moba_dec",
    )(phys, per_head(q), per_head(k_new), per_head(v_new),
      *([pool_k] * len(sel)), *([pool_v] * len(sel)))
    return out.reshape(nseq, BRANCH_W)


def kernel(x_prompt, x_sample, c_prompt, c_sample, cache_a_k, cache_a_v, cache_b_k, cache_b_v,
           cache_b_logf, page_table, w_ada, b_ada, g_norm1, g_norm2, g_final, w_in, b_f,
           w_a_o, w_b_o, w_out, w_gate, w_up, w_down):
    depth = w_ada.shape[0]
    assert depth == 1
    batch, seq, _ = x_prompt.shape
    nseq, dec_seq, _ = x_sample.shape
    assert batch == 1 and dec_seq == 1
    n_pool = cache_a_k.shape[1]
    n_pages = page_table.shape[1]
    past_len = n_pages * PAGE
    assert past_len % MOBA_BLOCK == 0

    w_in0 = w_in[0]
    n_qkv = 6 * BRANCH_W
    wqkv = w_in0[:, :n_qkv].astype(BF16)
    wf = jnp.pad(w_in0[:, n_qkv:n_qkv + N_HEADS], ((0, 0), (0, LANES - N_HEADS))).astype(BF16)
    wg = w_in0[:, n_qkv + N_HEADS:].astype(BF16)
    bf = jnp.pad(b_f[0], (0, LANES - N_HEADS)).reshape(1, LANES)
    wao = w_a_o[0].astype(BF16)
    wbo = w_b_o[0].astype(BF16)
    wout = w_out[0].astype(BF16)
    wgate = w_gate[0].astype(BF16)
    wup = w_up[0].astype(BF16)
    wdown = w_down[0].astype(BF16)
    g1 = g_norm1[0].reshape(1, D_MODEL)
    g2 = g_norm2[0].reshape(1, D_MODEL)
    gf = g_final.reshape(1, D_MODEL)
    inv8 = ROPE_THETA ** (-jnp.arange(ROT_HALF, dtype=F32) / ROT_HALF)
    lane = np.arange(LANES)
    inv = jnp.where(jnp.asarray((lane % HEAD_DIM) < ROT_DIM), inv8[lane % ROT_HALF], 0.0)
    inv = inv.reshape(1, LANES).astype(F32)

    pad = (-(batch + nseq)) % 8
    c_all = jnp.concatenate([c_prompt, c_sample, jnp.zeros((pad, D_MODEL), F32)], axis=0)
    mods = _ada(c_all, w_ada[0], b_ada[0])
    mp = [mods[0:1, j * D_MODEL:(j + 1) * D_MODEL] for j in range(6)]
    ms = [mods[batch:batch + nseq, j * D_MODEL:(j + 1) * D_MODEL] for j in range(6)]

    xp = x_prompt.reshape(seq, D_MODEL)
    qa, ka, va, qb, kb, vb, lf, sga, sgb = _inproj(
        xp, mp[0], mp[1], g1, wqkv, wf, wg, bf, inv, tm=256, pos0=0, pos_stride=1)
    xs = x_sample.reshape(nseq, D_MODEL)
    qa_s, ka_s, va_s, qb_s, kb_s, vb_s, lf_s, sga_s, sgb_s = _inproj(
        xs, ms[0], ms[1], g1, wqkv, wf, wg, bf, inv, tm=nseq, pos0=past_len, pos_stride=0)

    pool_t = lambda c: jnp.transpose(c[0], (0, 2, 3, 1))
    pool_ak, pool_av = pool_t(cache_a_k), pool_t(cache_a_v)
    pool_bk, pool_bv = pool_t(cache_b_k), pool_t(cache_b_v)
    pool_lft = jnp.transpose(cache_b_logf[0], (0, 2, 1))

    qaug, kaug, vaug, stats = _prep(qa, ka, va, qb, kb, vb, lf)
    tile_stats = [stats[:, r, :N_HEADS] for r in range(4)]
    o_p = _attn(qaug, kaug, vaug, tile_stats, tq=1024)
    y_p = _post(xp, o_p, sga, sgb, mp[2], mp[3], mp[4], mp[5], g2, gf,
                wao, wbo, wout, wgate, wup, wdown, tm=256)

    ob_s = _fox_dec(page_table, qb_s, kb_s, vb_s, lf_s[:, :N_HEADS],
                    pool_bk, pool_bv, pool_lft, npg=32)
    idx = _moba_sel(page_table, qa_s, pool_ak, npg=64)
    oa_s = _moba_dec(page_table, idx, qa_s, ka_s, va_s, pool_ak, pool_av)
    o_s = jnp.concatenate([oa_s, ob_s], axis=1)
    y_s = _post(xs, o_s, sga_s, sgb_s, ms[2], ms[3], ms[4], ms[5], g2, gf,
                wao, wbo, wout, wgate, wup, wdown, tm=nseq)

    return (y_p.reshape(batch, seq, D_MODEL),
            y_s.reshape(nseq, dec_seq, D_MODEL),
            ka.reshape(depth, batch, seq, N_HEADS, HEAD_DIM),
            va.reshape(depth, batch, seq, N_HEADS, HEAD_DIM),
            kb.reshape(depth, batch, seq, N_HEADS, HEAD_DIM),
            vb.reshape(depth, batch, seq, N_HEADS, HEAD_DIM),
            lf[:, :N_HEADS].reshape(depth, batch, seq, N_HEADS),
            ka_s.reshape(depth, nseq, dec_seq, N_HEADS, HEAD_DIM),
            va_s.reshape(depth, nseq, dec_seq, N_HEADS, HEAD_DIM),
            kb_s.reshape(depth, nseq, dec_seq, N_HEADS, HEAD_DIM),
            vb_s.reshape(depth, nseq, dec_seq, N_HEADS, HEAD_DIM),
            lf_s[:, :N_HEADS].reshape(depth, nseq, dec_seq, N_HEADS))
```
